```python
import jax, jax.numpy as jnp
from jax import lax
import numpy as np

D_MODEL = 1024
BATCH = 8
SEQ = 2048
DEPTH = 4
DEC_BATCH = 128
DEC_SEQ = 4
PAST_LEN = 16384
PAGE_SIZE = 128

N_MEM = 256
GROUP_W = 256
N_GROUPS = 5
D_MIX = N_GROUPS * GROUP_W
HEAD_DIM = 64
N_HEADS = GROUP_W // HEAD_DIM
A_LORA_W = 64
A_LORA_A = 64
A_SHIFT = 3 * GROUP_W + A_LORA_W + A_LORA_A
A_GN_EPS = 64e-5
B_DK = 32
B_QK = N_HEADS * B_DK
B_LORA_G = 16
B_GATE_NORM = 16.0
C_EXPAND = 64
C_F = N_HEADS * C_EXPAND
D_CONV_W = 3
E_HEADS = 4
CHUNK = 64
EPS = 1e-6
PROJ_WIDTHS = (A_SHIFT, GROUP_W,
               B_QK, B_QK, GROUP_W, B_LORA_G, GROUP_W,
               C_F, C_F, GROUP_W, GROUP_W,
               GROUP_W, GROUP_W, GROUP_W, GROUP_W,
               GROUP_W, GROUP_W)
D_PROJ = sum(PROJ_WIDTHS)

kernel_name = 'hybrid_rwkv7_gla_hgrn2_shortconv_memxattn_step'


def _split(a, widths):
    return jnp.split(a, np.cumsum(widths)[:-1].tolist(), axis=-1)


def _rmsnorm(x, g):
    xf = x.astype(jnp.float32)
    y = xf * lax.rsqrt(jnp.mean(xf * xf, axis=-1, keepdims=True) + EPS)
    return (y * g.astype(jnp.float32)).astype(x.dtype)


def _heads(a, d):
    return a.reshape(a.shape[0], a.shape[1], -1, d)


def _rwkv7_scan(r, w, k, v, kk, a, s0):
    def step(s, inp):
        r_t, w_t, k_t, v_t, kk_t, a_t = inp
        sa = jnp.einsum('bhvk,bhk->bhv', s, -kk_t)
        s = (s * w_t[:, :, None, :] + sa[..., None] * (kk_t * a_t)[:, :, None, :]
             + v_t[..., None] * k_t[:, :, None, :])
        return s, jnp.einsum('bhvk,bhk->bhv', s, r_t)
    xs = tuple(jnp.moveaxis(z, 1, 0) for z in (r, w, k, v, kk, a))
    s, o = lax.scan(step, s0, xs)
    return jnp.moveaxis(o, 0, 1), s


def _chunked_gla(q, k, v, log_g, s0):
    bsz, t, h, _ = q.shape
    dv = v.shape[-1]
    c = CHUNK if t % CHUNK == 0 else t
    n = t // c

    def chunks(z):
        return jnp.moveaxis(z.reshape(bsz, n, c, h, z.shape[-1]), 1, 0)

    causal = jnp.tril(jnp.ones((c, c), dtype=bool))[None, :, :, None, None]

    def step(s, blk):
        qc, kc, vc, gc = blk
        b = jnp.cumsum(gc, axis=1)
        o_inter = jnp.einsum('bihk,bhkv->bihv', qc * jnp.exp(b), s)
        dec = jnp.exp(jnp.where(causal, b[:, :, None] - b[:, None, :], -jnp.inf))
        att = jnp.einsum('bihk,bjhk,bijhk->bhij', qc, kc, dec)
        o_intra = jnp.einsum('bhij,bjhv->bihv', att, vc)
        b_last = b[:, -1]
        s = (jnp.exp(b_last)[..., None] * s
             + jnp.einsum('bjhk,bjhv->bhkv', kc * jnp.exp(b_last[:, None] - b), vc))
        return s, o_inter + o_intra

    s, o = lax.scan(step, s0, (chunks(q), chunks(k), chunks(v), chunks(log_g)))
    return jnp.moveaxis(o, 0, 1).reshape(bsz, t, h, dv), s


def _layer(x, mem_k, mem_v, s_wkv, s_shift, s_gla, s_hgrn, s_conv, lb, lw):
    f32 = jnp.float32
    bsz, t, _ = x.shape
    u = _rmsnorm(x, lw['ln_pre'])
    proj = (u @ lw['w_in']).astype(f32)
    (a_x, a_z, b_q, b_k, b_v, b_gl, b_z, c_q, c_f, c_i, c_z,
     d_b, d_c, d_x, d_z, e_q, e_z) = _split(proj, PROJ_WIDTHS)

    prev = jnp.concatenate([s_shift.astype(f32)[:, None], a_x[:, :-1]], axis=1)
    a_mix = a_x + (prev - a_x) * lw['a_mu']
    r, k, v, wl, al = _split(a_mix, (GROUP_W, GROUP_W, GROUP_W, A_LORA_W, A_LORA_A))
    w_raw = -jax.nn.softplus(-(lw['a_w0'] + jnp.tanh(wl) @ lw['a_w2'])) - 0.5
    decay = jnp.exp(-jnp.exp(w_raw))
    a = jax.nn.sigmoid(lw['a_a0'] + al @ lw['a_a2'])
    kk = _heads(k * lw['a_kk'], HEAD_DIM)
    kk = kk * lax.rsqrt(jnp.maximum(jnp.sum(kk * kk, axis=-1, keepdims=True), 1e-24))
    k = k * (1.0 + (a - 1.0) * lw['a_ka'])
    rh, kh, vh = _heads(r, HEAD_DIM), _heads(k, HEAD_DIM), _heads(v, HEAD_DIM)
    o_a, s_wkv_new = _rwkv7_scan(rh, _heads(decay, HEAD_DIM), kh, vh, kk,
                                 _heads(a, HEAD_DIM), s_wkv.astype(f32))
    mu = jnp.mean(o_a, axis=-1, keepdims=True)
    var = jnp.mean(jnp.square(o_a - mu), axis=-1, keepdims=True)
    o_a = ((o_a - mu) * lax.rsqrt(var + A_GN_EPS)).reshape(bsz, t, GROUP_W) * lw['a_gn_g'] + lw['a_gn_b']
    bonus = jnp.sum(rh * kh * lw['a_rk'], axis=-1, keepdims=True) * vh
    out_a = (o_a + bonus.reshape(bsz, t, GROUP_W)) * jax.nn.silu(a_z)

    qb = _heads(b_q, B_DK) * (B_DK ** -0.5)
    kb = _heads(b_k, B_DK)
    vb = _heads(b_v, HEAD_DIM)
    gb = _heads(jax.nn.log_sigmoid(b_gl @ lw['b_g2'] + lw['b_g2_bias']) / B_GATE_NORM, B_DK)
    o_b, s_gla_new = _chunked_gla(qb, kb, vb, gb, s_gla.astype(f32))
    o_b = o_b * lax.rsqrt(jnp.mean(o_b * o_b, axis=-1, keepdims=True) + EPS)
    out_b = o_b.reshape(bsz, t, GROUP_W) * lw['b_gn_g'] * jax.nn.silu(b_z)

    qc = _heads(jax.nn.silu(c_q), C_EXPAND) * (C_EXPAND ** -0.5)
    log_f = jnp.logaddexp(jnp.log(lb), jnp.log1p(-lb) + jax.nn.log_sigmoid(c_f))
    kc = (1.0 - lb) * jax.nn.sigmoid(-c_f)
    o_c, s_hgrn_new = _chunked_gla(qc, _heads(kc, C_EXPAND), _heads(c_i, HEAD_DIM),
                                   _heads(log_f, C_EXPAND), s_hgrn.astype(f32))
    out_c = _rmsnorm(o_c.reshape(bsz, t, GROUP_W), lw['c_gn_g']) * jax.nn.silu(c_z)

    cu = d_c * d_x
    full = jnp.concatenate([s_conv.astype(f32), cu], axis=1)
    cw = lw['d_conv_w']
    conv = full[:, 0:t] * cw[0]
    for j in range(1, D_CONV_W):
        conv = conv + full[:, j:j + t] * cw[j]
    s_conv_new = full[:, -(D_CONV_W - 1):]
    out_d = d_b * conv * jax.nn.silu(d_z)

    qe = _heads(e_q, HEAD_DIM) * (HEAD_DIM ** -0.5)
    sc = jnp.einsum('bthd,bmhd->bhtm', qe, mem_k.astype(f32))
    pr = jax.nn.softmax(sc, axis=-1)
    oe = jnp.einsum('bhtm,bmhd->bthd', pr, mem_v.astype(f32)).reshape(bsz, t, GROUP_W)
    out_e = oe * jax.nn.silu(e_z)

    mix = jnp.concatenate([out_a, out_b, out_c, out_d, out_e], axis=-1).astype(x.dtype)
    y = x + _rmsnorm(mix @ lw['w_out'], lw['ln_post'])
    dt = x.dtype
    return y, (s_wkv_new.astype(dt), a_x[:, -1].astype(dt), s_gla_new.astype(dt),
               s_hgrn_new.astype(dt), s_conv_new.astype(dt))


def setup_inputs(seed: int = 0) -> dict:
    key = jax.random.key(seed)
    ks = jax.random.split(key, 40)
    nrm = jax.random.normal
    f32 = jnp.float32
    d = {}
    d['x_prompt'] = nrm(ks[0], (BATCH, SEQ, D_MODEL), f32)
    d['x_sample'] = nrm(ks[1], (DEC_BATCH, DEC_SEQ, D_MODEL), f32)
    d['mem_prompt'] = nrm(ks[2], (BATCH, N_MEM, D_MODEL), f32)
    d['state_a_wkv'] = 0.3 * nrm(ks[3], (DEPTH, DEC_BATCH, N_HEADS, HEAD_DIM, HEAD_DIM), f32)
    d['state_a_shift'] = nrm(ks[4], (DEPTH, DEC_BATCH, A_SHIFT), f32)
    d['state_b_gla'] = 0.3 * nrm(ks[5], (DEPTH, DEC_BATCH, N_HEADS, B_DK, HEAD_DIM), f32)
    d['state_c_hgrn'] = 0.3 * nrm(ks[6], (DEPTH, DEC_BATCH, N_HEADS, C_EXPAND, HEAD_DIM), f32)
    d['state_d_conv'] = nrm(ks[7], (DEPTH, DEC_BATCH, D_CONV_W - 1, GROUP_W), f32)
    d['cache_mem_k'] = nrm(ks[8], (DEPTH, DEC_BATCH, N_MEM, E_HEADS, HEAD_DIM), f32)
    d['cache_mem_v'] = nrm(ks[9], (DEPTH, DEC_BATCH, N_MEM, E_HEADS, HEAD_DIM), f32)
    d['ln_pre'] = 1.0 + 0.05 * nrm(ks[10], (DEPTH, D_MODEL), f32)
    d['ln_post'] = 1.0 + 0.05 * nrm(ks[11], (DEPTH, D_MODEL), f32)
    d['w_in'] = nrm(ks[12], (DEPTH, D_MODEL, D_PROJ), f32) * D_MODEL ** -0.5
    d['w_out'] = nrm(ks[13], (DEPTH, D_MIX, D_MODEL), f32) * D_MIX ** -0.5
    d['a_mu'] = jax.random.uniform(ks[14], (DEPTH, A_SHIFT), f32)
    d['a_w0'] = -2.0 + nrm(ks[15], (DEPTH, GROUP_W), f32)
    d['a_w2'] = 0.5 * nrm(ks[16], (DEPTH, A_LORA_W, GROUP_W), f32) * A_LORA_W ** -0.5
    d['a_a0'] = 0.5 * nrm(ks[17], (DEPTH, GROUP_W), f32)
    d['a_a2'] = nrm(ks[18], (DEPTH, A_LORA_A, GROUP_W), f32) * A_LORA_A ** -0.5
    d['a_kk'] = 0.85 + 0.05 * nrm(ks[19], (DEPTH, GROUP_W), f32)
    d['a_ka'] = 1.0 + 0.05 * nrm(ks[20], (DEPTH, GROUP_W), f32)
    d['a_rk'] = 0.1 * nrm(ks[21], (DEPTH, N_HEADS, HEAD_DIM), f32)
    d['a_gn_g'] = 1.0 + 0.05 * nrm(ks[22], (DEPTH, GROUP_W), f32)
    d['a_gn_b'] = 0.01 * nrm(ks[23], (DEPTH, GROUP_W), f32)
    d['b_g2'] = nrm(ks[24], (DEPTH, B_LORA_G, B_QK), f32) * B_LORA_G ** -0.5
    d['b_g2_bias'] = 0.1 * nrm(ks[25], (DEPTH, B_QK), f32)
    d['b_gn_g'] = 1.0 + 0.05 * nrm(ks[26], (DEPTH, GROUP_W), f32)
    d['c_lb_logits'] = 0.1 * nrm(ks[27], (DEPTH, C_F), f32)
    d['c_gn_g'] = 1.0 + 0.05 * nrm(ks[28], (DEPTH, GROUP_W), f32)
    d['d_conv_w'] = nrm(ks[29], (DEPTH, D_CONV_W, GROUP_W), f32) * D_CONV_W ** -0.5
    d['e_mem_g'] = 1.0 + 0.05 * nrm(ks[30], (DEPTH, D_MODEL), f32)
    d['e_wk'] = nrm(ks[31], (DEPTH, D_MODEL, GROUP_W), f32) * D_MODEL ** -0.5
    d['e_wv'] = nrm(ks[32], (DEPTH, D_MODEL, GROUP_W), f32) * D_MODEL ** -0.5
    return d


def reference(x_prompt, x_sample, mem_prompt, state_a_wkv, state_a_shift, state_b_gla,
              state_c_hgrn, state_d_conv, cache_mem_k, cache_mem_v, ln_pre, ln_post, w_in, w_out,
              a_mu, a_w0, a_w2, a_a0, a_a2, a_kk, a_ka, a_rk, a_gn_g, a_gn_b, b_g2, b_g2_bias,
              b_gn_g, c_lb_logits, c_gn_g, d_conv_w, e_mem_g, e_wk, e_wv):
    f32 = jnp.float32
    lb_cum = jnp.cumsum(jax.nn.softmax(c_lb_logits.astype(f32), axis=0), axis=0)
    lb_all = lb_cum - lb_cum[:1]
    bp = x_prompt.shape[0]
    dt = x_prompt.dtype
    yp, ys = x_prompt, x_sample
    pw, pa, pb, pc, pd, pmk, pmv = [], [], [], [], [], [], []
    sw, sa, sb, sc, sd = [], [], [], [], []
    for l in range(DEPTH):
        lw = {'ln_pre': ln_pre[l], 'ln_post': ln_post[l], 'w_in': w_in[l], 'w_out': w_out[l],
              'a_mu': a_mu[l], 'a_w0': a_w0[l], 'a_w2': a_w2[l], 'a_a0': a_a0[l], 'a_a2': a_a2[l],
              'a_kk': a_kk[l], 'a_ka': a_ka[l], 'a_rk': a_rk[l], 'a_gn_g': a_gn_g[l], 'a_gn_b': a_gn_b[l],
              'b_g2': b_g2[l], 'b_g2_bias': b_g2_bias[l], 'b_gn_g': b_gn_g[l],
              'c_gn_g': c_gn_g[l], 'd_conv_w': d_conv_w[l]}
        mem_n = _rmsnorm(mem_prompt, e_mem_g[l])
        mk = (mem_n @ e_wk[l]).reshape(bp, N_MEM, E_HEADS, HEAD_DIM)
        mv = (mem_n @ e_wv[l]).reshape(bp, N_MEM, E_HEADS, HEAD_DIM)
        yp, ps = _layer(yp, mk, mv,
                        jnp.zeros((bp, N_HEADS, HEAD_DIM, HEAD_DIM), dt),
                        jnp.zeros((bp, A_SHIFT), dt),
                        jnp.zeros((bp, N_HEADS, B_DK, HEAD_DIM), dt),
                        jnp.zeros((bp, N_HEADS, C_EXPAND, HEAD_DIM), dt),
                        jnp.zeros((bp, D_CONV_W - 1, GROUP_W), dt),
                        lb_all[l], lw)
        ys, ss = _layer(ys, cache_mem_k[l], cache_mem_v[l], state_a_wkv[l], state_a_shift[l],
                        state_b_gla[l], state_c_hgrn[l], state_d_conv[l], lb_all[l], lw)
        pw.append(ps[0]); pa.append(ps[1]); pb.append(ps[2]); pc.append(ps[3]); pd.append(ps[4])
        pmk.append(mk); pmv.append(mv)
        sw.append(ss[0]); sa.append(ss[1]); sb.append(ss[2]); sc.append(ss[3]); sd.append(ss[4])
    p_a_wkv, p_a_shift, p_b_gla = jnp.stack(pw), jnp.stack(pa), jnp.stack(pb)
    p_c_hgrn, p_d_conv = jnp.stack(pc), jnp.stack(pd)
    p_mem_k, p_mem_v = jnp.stack(pmk), jnp.stack(pmv)
    s_a_wkv, s_a_shift, s_b_gla = jnp.stack(sw), jnp.stack(sa), jnp.stack(sb)
    s_c_hgrn, s_d_conv = jnp.stack(sc), jnp.stack(sd)
    return (yp, ys, p_a_wkv, p_a_shift, p_b_gla, p_c_hgrn, p_d_conv, p_mem_k, p_mem_v,
            s_a_wkv, s_a_shift, s_b_gla, s_c_hgrn, s_d_conv)
```

```python
import functools

import numpy as np
import jax
import jax.numpy as jnp
from jax import lax
from jax.experimental import pallas as pl
from jax.experimental.pallas import tpu as pltpu

F32 = jnp.float32
BF16 = jnp.bfloat16

D_MODEL = 1024
DEPTH = 4
N_MEM = 256
GROUP_W = 256
N_HEADS = 4
HEAD_DIM = 64
A_LORA = 64
A_SHIFT = 3 * GROUP_W + 2 * A_LORA
A_GN_EPS = 64e-5
B_DK = 32
B_QK = N_HEADS * B_DK
B_LORA_G = 16
B_GATE_NORM = 16.0
C_EXPAND = 64
C_F = N_HEADS * C_EXPAND
D_CONV_W = 3
D_MIX = 5 * GROUP_W
EPS = 1e-6
LANES = 128
SUBLANES = 8

_REF_WIDTHS = (A_SHIFT, GROUP_W, B_QK, B_QK, GROUP_W, B_LORA_G, GROUP_W, C_F, C_F, GROUP_W, GROUP_W,
               GROUP_W, GROUP_W, GROUP_W, GROUP_W, GROUP_W, GROUP_W)
_REF_OFF = np.concatenate([[0], np.cumsum(_REF_WIDTHS)]).tolist()
_NAMES = ("a_x", "a_z", "b_q", "b_k", "b_v", "b_gl", "b_z", "c_q", "c_f", "c_i", "c_z",
          "d_b", "d_c", "d_x", "d_z", "e_q", "e_z")
_PACK_ORDER = ("a_x", "a_z", "b_q", "b_k", "b_v", "b_z", "c_q", "c_f", "c_i", "c_z",
               "d_b", "d_c", "d_x", "d_z", "e_q", "e_z", "b_gl")
_OFF = {}
_o = 0
for _n in _PACK_ORDER:
    _w = _REF_WIDTHS[_NAMES.index(_n)]
    _OFF[_n] = (_o, _o + _w)
    _o += -(-_w // LANES) * LANES
D_PROJ_P = _o

PROMPT_CHUNK = 64
SAMPLE_CHUNK = 8
ROW_TILE_IN = 256
ROW_TILE_OUT = 512
VMEM_LIMIT = 48 * 1024 * 1024


def _bf(x):
    return x.astype(BF16)


def _mm(a, b):
    return lax.dot_general(_bf(a), _bf(b), (((1,), (0,)), ((), ())), preferred_element_type=F32)


def _mm_nt(a, b):
    return lax.dot_general(_bf(a), _bf(b), (((1,), (1,)), ((), ())), preferred_element_type=F32)


def _mm_tn(a, b):
    return lax.dot_general(_bf(a), _bf(b), (((0,), (0,)), ((), ())), preferred_element_type=F32)


def _rows(shape):
    return lax.broadcasted_iota(jnp.int32, shape, 0)


def _cols(shape):
    return lax.broadcasted_iota(jnp.int32, shape, 1)


def _shift_rows(x, s):
    if s == 0:
        return x
    return jnp.where(_rows(x.shape) >= s, pltpu.roll(x, s, axis=0), 0.0)


def _cumsum_rows(x):
    s = 1
    while s < x.shape[0]:
        x = x + _shift_rows(x, s)
        s *= 2
    return x


def _softplus(x):
    return jnp.maximum(x, 0.0) + jnp.log1p(jnp.exp(-jnp.abs(x)))


def _log_sigmoid(x):
    return -_softplus(-x)


def _silu(x):
    return x * jax.nn.sigmoid(x)


def _head_sum(x):
    parts = []
    for h in range(N_HEADS):
        s = jnp.sum(x[:, h * HEAD_DIM:(h + 1) * HEAD_DIM], axis=-1, keepdims=True)
        parts.append(jnp.broadcast_to(s, (x.shape[0], HEAD_DIM)))
    return jnp.concatenate(parts, axis=-1)


def _log2(n):
    return int(np.log2(n))


def _rms_matmul_kernel(x_ref, g_ref, w_ref, *o_refs):
    x = x_ref[...]
    u = _bf(x * lax.rsqrt(jnp.mean(x * x, axis=-1, keepdims=True) + EPS) * g_ref[...])
    off = 0
    for o_ref in o_refs:
        n = o_ref.shape[-1]
        o_ref[...] = jnp.dot(u, w_ref[:, off:off + n], preferred_element_type=F32)
        off += n


def _rms_matmul(x, g, w, out_widths, name):
    n, d = x.shape
    tm = min(ROW_TILE_IN, n)
    return pl.pallas_call(
        _rms_matmul_kernel,
        grid=(n // tm,),
        in_specs=[pl.BlockSpec((tm, d), lambda i: (i, 0)),
                  pl.BlockSpec((1, d), lambda i: (0, 0)),
                  pl.BlockSpec(w.shape, lambda i: (0, 0))],
        out_specs=[pl.BlockSpec((tm, ow), lambda i: (i, 0)) for ow in out_widths],
        out_shape=[jax.ShapeDtypeStruct((n, ow), F32) for ow in out_widths],
        compiler_params=pltpu.CompilerParams(dimension_semantics=("arbitrary",),
                                             vmem_limit_bytes=VMEM_LIMIT),
        name=name,
    )(x, g.reshape(1, d), w)


def _outproj_kernel(mix_ref, w_ref, g_ref, x_ref, y_ref):
    z = jnp.dot(mix_ref[...], w_ref[...], preferred_element_type=F32)
    y_ref[...] = x_ref[...] + z * lax.rsqrt(jnp.mean(z * z, axis=-1, keepdims=True) + EPS) * g_ref[...]


def _outproj(mix, w, g, x, name):
    n, d = x.shape
    tm = min(ROW_TILE_OUT, n)
    return pl.pallas_call(
        _outproj_kernel,
        grid=(n // tm,),
        in_specs=[pl.BlockSpec((tm, D_MIX), lambda i: (i, 0)),
                  pl.BlockSpec((D_MIX, d), lambda i: (0, 0)),
                  pl.BlockSpec((1, d), lambda i: (0, 0)),
                  pl.BlockSpec((tm, d), lambda i: (i, 0))],
        out_specs=pl.BlockSpec((tm, d), lambda i: (i, 0)),
        out_shape=jax.ShapeDtypeStruct((n, d), F32),
        compiler_params=pltpu.CompilerParams(dimension_semantics=("arbitrary",),
                                             vmem_limit_bytes=VMEM_LIMIT),
        name=name,
    )(mix, w, g.reshape(1, d), x)


def _rwkv_head(r, v, k, at, rt, bt, kt, bh, kh, e_end_c, s0):
    c = r.shape[0]
    ar = jnp.concatenate([at, rt], axis=0)
    gb = _mm_nt(ar, bt)
    gk = _mm_nt(ar, kt)
    ri, ci = _rows((c, c)), _cols((c, c))
    strict, incl = ci < ri, ci <= ri
    a_ab = jnp.where(strict, gb[:c], 0.0)
    a_ak = jnp.where(strict, gk[:c], 0.0)
    a_rb = jnp.where(incl, gb[c:], 0.0)
    a_rk = jnp.where(incl, gk[c:], 0.0)
    w0 = _mm_nt(ar, s0)
    u = w0[:c] + _mm(a_ak, v)
    p, n = a_ab, 1
    while n < c:
        u = u + _mm(p, u)
        n *= 2
        if n < c:
            p = _mm(p, p)
    o = w0[c:] + _mm(a_rb, u) + _mm(a_rk, v)
    s_new = s0 * e_end_c + _mm_tn(u, bh) + _mm_tn(v, kh)
    return o, s_new


def _gla_chunk(q, k, v, lg, st, dk):
    c, wk = q.shape
    b = _cumsum_rows(lg)
    b_c = b[c - 1:c, :]
    o = _mm_nt(q * jnp.exp(b), st)
    rows = _rows((c, wk))

    half = c // 2
    att = None
    hj_head = lax.shift_right_logical(_rows((N_HEADS * c, wk)), _log2(c))
    k_head = lax.shift_right_logical(_cols((N_HEADS * c, wk)), _log2(dk))
    kmask = hj_head == k_head
    ai, aj = _rows((c, N_HEADS * c)), jnp.bitwise_and(_cols((c, N_HEADS * c)), c - 1)
    while half >= SUBLANES:
        blk = 2 * half
        nb = c // blk
        mid = jnp.broadcast_to(b.reshape(nb, blk, wk)[:, half - 1:half, :], (nb, blk, wk)).reshape(c, wk)
        second = jnp.bitwise_and(rows, blk - 1) >= half
        ql = jnp.where(second, q * jnp.exp(jnp.where(second, b - mid, 0.0)), 0.0)
        kl = jnp.where(second, 0.0, k * jnp.exp(jnp.where(second, 0.0, mid - b)))
        kbd = jnp.where(kmask, jnp.concatenate([kl] * N_HEADS, axis=0), 0.0)
        lvl = _mm_nt(ql, kbd)
        same = lax.shift_right_logical(ai, _log2(blk)) == lax.shift_right_logical(aj, _log2(blk))
        lvl = jnp.where(same, lvl, 0.0)
        att = lvl if att is None else att + lvl
        half //= 2
    if att is not None:
        vj_head = lax.shift_right_logical(_rows((N_HEADS * c, GROUP_W)), _log2(c))
        vv_head = lax.shift_right_logical(_cols((N_HEADS * c, GROUP_W)), _log2(HEAD_DIM))
        vbd = jnp.where(vj_head == vv_head, jnp.concatenate([v] * N_HEADS, axis=0), 0.0)
        o = o + _mm(att, vbd)

    ones_bd = (lax.shift_right_logical(_rows((wk, GROUP_W)), _log2(dk)) ==
               lax.shift_right_logical(_cols((wk, GROUP_W)), _log2(HEAD_DIM))).astype(F32)
    in_blk = jnp.bitwise_and(rows, SUBLANES - 1)
    terms, vds = [], []
    for d in range(SUBLANES):
        ok = in_blk >= d
        kd, bd = _shift_rows(k, d), _shift_rows(b, d)
        terms.append(jnp.where(ok, q * kd * jnp.exp(jnp.where(ok, b - bd, 0.0)), 0.0))
        vds.append(_shift_rows(v, d))
    s_all = _mm(jnp.concatenate(terms, axis=0), ones_bd)
    for d in range(SUBLANES):
        o = o + s_all[d * c:(d + 1) * c] * vds[d]

    sj_head = lax.shift_right_logical(_rows((GROUP_W, wk)), _log2(HEAD_DIM))
    sk_head = lax.shift_right_logical(_cols((GROUP_W, wk)), _log2(dk))
    st_new = st * jnp.exp(b_c) + jnp.where(sj_head == sk_head, _mm_tn(v, k * jnp.exp(b_c - b)), 0.0)
    return o, st_new


def _mixers_kernel(t_valid, nc,
                   proj_ref, mk_ref, mv_ref, wkv0_ref, shift0_ref, gla0_ref, hgrn0_ref, conv0_ref,
                   mu_ref, w0_ref, w2_ref, a0_ref, a2_ref, akk_ref, aka_ref, ark_ref, agg_ref, agb_ref,
                   g2_ref, g2b_ref, bgg_ref, lb_ref, cgg_ref, cw_ref,
                   mix_ref, wkv_ref, shift_ref, gla_ref, hgrn_ref, conv_ref,
                   kbd_ref, vbd_ref):
    c = proj_ref.shape[0]
    step = pl.program_id(1)

    @pl.when(step == 0)
    def _():
        wkv_ref[...] = wkv0_ref[...]
        shift_ref[...] = shift0_ref[...]
        gla_ref[...] = gla0_ref[...]
        hgrn_ref[...] = hgrn0_ref[...]
        conv_ref[...] = conv0_ref[...]
        mh = lax.shift_right_logical(_rows((N_HEADS * N_MEM, GROUP_W)), _log2(N_MEM))
        dh = lax.shift_right_logical(_cols((N_HEADS * N_MEM, GROUP_W)), _log2(HEAD_DIM))
        same = mh == dh
        kbd_ref[...] = _bf(jnp.where(same, jnp.concatenate([mk_ref[...]] * N_HEADS, axis=0), 0.0))
        vbd_ref[...] = _bf(jnp.where(same, jnp.concatenate([mv_ref[...]] * N_HEADS, axis=0), 0.0))

    def seg(name):
        lo, hi = _OFF[name]
        return proj_ref[:, lo:hi]

    row1 = _rows((c, 1))
    pad = t_valid < c
    valid = row1 < t_valid
    last = (t_valid if pad else c) - 1

    ax, az = seg("a_x"), seg("a_z")
    prev = _shift_rows(ax, 1) + jnp.where(row1 == 0, shift_ref[...], 0.0)
    amix = ax + (prev - ax) * mu_ref[...]
    r = amix[:, 0:GROUP_W]
    k = amix[:, GROUP_W:2 * GROUP_W]
    v = amix[:, 2 * GROUP_W:3 * GROUP_W]
    wl = amix[:, 3 * GROUP_W:3 * GROUP_W + A_LORA]
    al = amix[:, 3 * GROUP_W + A_LORA:A_SHIFT]
    w_raw = -_softplus(-(w0_ref[...] + _mm(jnp.tanh(wl), w2_ref[...]))) - 0.5
    lw = -jnp.exp(w_raw)
    a = jax.nn.sigmoid(a0_ref[...] + _mm(al, a2_ref[...]))
    kk = k * akk_ref[...]
    kk = kk * lax.rsqrt(jnp.maximum(_head_sum(kk * kk), 1e-24))
    k = k * (1.0 + (a - 1.0) * aka_ref[...])
    bb = kk * a
    if pad:
        lw = jnp.where(valid, lw, 0.0)
        bb = jnp.where(valid, bb, 0.0)
        k = jnp.where(valid, k, 0.0)
    cum = _cumsum_rows(lw)
    cum_c = cum[c - 1:c, :]
    e_cum, e_inv, e_end = jnp.exp(cum), jnp.exp(-cum), jnp.exp(cum_c - cum)
    at = -kk * jnp.exp(cum - lw)
    rt = r * e_cum
    bt, kt = bb * e_inv, k * e_inv
    bh, kh = bb * e_end, k * e_end
    e_end_c = jnp.exp(cum_c)
    o_heads = []
    for h in range(N_HEADS):
        sl = slice(h * HEAD_DIM, (h + 1) * HEAD_DIM)
        o_h, s_h = _rwkv_head(r[:, sl], v[:, sl], k[:, sl], at[:, sl], rt[:, sl], bt[:, sl], kt[:, sl],
                              bh[:, sl], kh[:, sl], e_end_c[:, sl], wkv_ref[h])
        wkv_ref[h] = s_h
        o_heads.append(o_h)
    o_a = jnp.concatenate(o_heads, axis=-1)
    mean = _head_sum(o_a) * (1.0 / HEAD_DIM)
    var = _head_sum(jnp.square(o_a - mean)) * (1.0 / HEAD_DIM)
    o_a = (o_a - mean) * lax.rsqrt(var + A_GN_EPS) * agg_ref[...] + agb_ref[...]
    bonus = _head_sum(r * k * ark_ref[...]) * v
    out_a = (o_a + bonus) * _silu(az)
    shift_ref[...] = ax[last:last + 1, :]

    bq = seg("b_q") * (B_DK ** -0.5)
    bk, bv, bz = seg("b_k"), seg("b_v"), seg("b_z")
    lo, _ = _OFF["b_gl"]
    lg = _log_sigmoid(_mm(proj_ref[:, lo:lo + LANES], g2_ref[...]) + g2b_ref[...]) * (1.0 / B_GATE_NORM)
    if pad:
        lg = jnp.where(valid, lg, 0.0)
        bk = jnp.where(valid, bk, 0.0)
    o_b, gla_new = _gla_chunk(bq, bk, bv, lg, gla_ref[...], B_DK)
    gla_ref[...] = gla_new
    o_b = o_b * lax.rsqrt(_head_sum(o_b * o_b) * (1.0 / HEAD_DIM) + EPS)
    out_b = o_b * bgg_ref[...] * _silu(bz)

    lb = lb_ref[...]
    cq = _silu(seg("c_q")) * (C_EXPAND ** -0.5)
    cf, ci, cz = seg("c_f"), seg("c_i"), seg("c_z")
    la, lbb = jnp.log(lb), jnp.log1p(-lb) + _log_sigmoid(cf)
    log_f = jnp.maximum(la, lbb) + jnp.log1p(jnp.exp(-jnp.abs(la - lbb)))
    kc = (1.0 - lb) * jax.nn.sigmoid(-cf)
    if pad:
        log_f = jnp.where(valid, log_f, 0.0)
        kc = jnp.where(valid, kc, 0.0)
    o_c, hgrn_new = _gla_chunk(cq, kc, ci, log_f, hgrn_ref[...], C_EXPAND)
    hgrn_ref[...] = hgrn_new
    o_c = o_c * lax.rsqrt(jnp.mean(o_c * o_c, axis=-1, keepdims=True) + EPS) * cgg_ref[...]
    out_c = o_c * _silu(cz)

    cu = seg("d_c") * seg("d_x")
    p1 = _shift_rows(cu, 1) + jnp.where(row1 == 0, conv_ref[1:2, :], 0.0)
    p2 = (_shift_rows(cu, 2) + jnp.where(row1 == 0, conv_ref[0:1, :], 0.0)
          + jnp.where(row1 == 1, conv_ref[1:2, :], 0.0))
    conv = p2 * cw_ref[0:1, :] + p1 * cw_ref[1:2, :] + cu * cw_ref[2:3, :]
    out_d = seg("d_b") * conv * _silu(seg("d_z"))
    conv_ref[...] = cu[last - 1:last + 1, :]

    qe = seg("e_q") * (HEAD_DIM ** -0.5)
    sc = _mm_nt(qe, kbd_ref[...])
    prs = []
    for h in range(N_HEADS):
        s_h = sc[:, h * N_MEM:(h + 1) * N_MEM]
        e_h = jnp.exp(s_h - jnp.max(s_h, axis=-1, keepdims=True))
        prs.append(e_h / jnp.sum(e_h, axis=-1, keepdims=True))
    oe = _mm(jnp.concatenate(prs, axis=-1), vbd_ref[...])
    out_e = oe * _silu(seg("e_z"))

    mix_ref[...] = _bf(jnp.concatenate([out_a, out_b, out_c, out_d, out_e], axis=-1))


def _mixers(proj, mk, mv, wkv0, shift0, gla0, hgrn0, conv0, params, chunk, t_valid, name):
    s, t, _ = proj.shape
    nc = t // chunk
    seq3 = lambda i, j: (i, 0, 0)
    seq4 = lambda i, j: (i, 0, 0, 0)
    par = lambda i, j: (0, 0)
    in_specs = [pl.BlockSpec((None, chunk, D_PROJ_P), lambda i, j: (i, j, 0)),
                pl.BlockSpec((None, N_MEM, GROUP_W), seq3),
                pl.BlockSpec((None, N_MEM, GROUP_W), seq3),
                pl.BlockSpec((None, N_HEADS, HEAD_DIM, HEAD_DIM), seq4),
                pl.BlockSpec((None, 1, A_SHIFT), seq3),
                pl.BlockSpec((None, GROUP_W, B_QK), seq3),
                pl.BlockSpec((None, GROUP_W, C_F), seq3),
                pl.BlockSpec((None, D_CONV_W - 1, GROUP_W), seq3)]
    in_specs += [pl.BlockSpec(p.shape, par) for p in params]
    out_specs = [pl.BlockSpec((None, chunk, D_MIX), lambda i, j: (i, j, 0)),
                 pl.BlockSpec((None, N_HEADS, HEAD_DIM, HEAD_DIM), seq4),
                 pl.BlockSpec((None, 1, A_SHIFT), seq3),
                 pl.BlockSpec((None, GROUP_W, B_QK), seq3),
                 pl.BlockSpec((None, GROUP_W, C_F), seq3),
                 pl.BlockSpec((None, D_CONV_W - 1, GROUP_W), seq3)]
    out_shape = [jax.ShapeDtypeStruct((s, t, D_MIX), BF16),
                 jax.ShapeDtypeStruct((s, N_HEADS, HEAD_DIM, HEAD_DIM), F32),
                 jax.ShapeDtypeStruct((s, 1, A_SHIFT), F32),
                 jax.ShapeDtypeStruct((s, GROUP_W, B_QK), F32),
                 jax.ShapeDtypeStruct((s, GROUP_W, C_F), F32),
                 jax.ShapeDtypeStruct((s, D_CONV_W - 1, GROUP_W), F32)]
    return pl.pallas_call(
        functools.partial(_mixers_kernel, t_valid, nc),
        grid=(s, nc),
        in_specs=in_specs,
        out_specs=out_specs,
        out_shape=out_shape,
        scratch_shapes=[pltpu.VMEM((N_HEADS * N_MEM, GROUP_W), BF16),
                        pltpu.VMEM((N_HEADS * N_MEM, GROUP_W), BF16)],
        compiler_params=pltpu.CompilerParams(dimension_semantics=("arbitrary", "arbitrary"),
                                             vmem_limit_bytes=VMEM_LIMIT),
        name=name,
    )(proj, mk, mv, wkv0, shift0, gla0, hgrn0, conv0, *params)


def _to_blockdiag_t(s):
    n, h, dk, dv = s.shape
    eye = jnp.eye(h, dtype=s.dtype)
    return jnp.einsum("bhkv,hg->bhvgk", s, eye).reshape(n, h * dv, h * dk)


def _from_blockdiag_t(st, dk):
    n = st.shape[0]
    x = st.reshape(n, N_HEADS, HEAD_DIM, N_HEADS, dk)
    idx = jnp.arange(N_HEADS)
    x = x[:, idx, :, idx, :]
    return jnp.transpose(x, (1, 0, 3, 2))


def kernel(x_prompt, x_sample, mem_prompt, state_a_wkv, state_a_shift, state_b_gla, state_c_hgrn,
           state_d_conv, cache_mem_k, cache_mem_v, ln_pre, ln_post, w_in, w_out, a_mu, a_w0, a_w2,
           a_a0, a_a2, a_kk, a_ka, a_rk, a_gn_g, a_gn_b, b_g2, b_g2_bias, b_gn_g, c_lb_logits,
           c_gn_g, d_conv_w, e_mem_g, e_wk, e_wv):
    bp, tp, _ = x_prompt.shape
    bs, ts, _ = x_sample.shape
    assert tp % PROMPT_CHUNK == 0 and ts <= SAMPLE_CHUNK and ts >= D_CONV_W - 1

    cols = [w_in[:, :, _REF_OFF[_NAMES.index(n)]:_REF_OFF[_NAMES.index(n) + 1]] for n in _PACK_ORDER]
    cols.append(jnp.zeros((DEPTH, D_MODEL, D_PROJ_P - w_in.shape[-1]), w_in.dtype))
    w_in_p = _bf(jnp.concatenate(cols, axis=-1))
    w_out_b = _bf(w_out)
    w_kv = _bf(jnp.concatenate([e_wk, e_wv], axis=-1))
    g2_p = jnp.concatenate([b_g2, jnp.zeros((DEPTH, LANES - B_LORA_G, B_QK), b_g2.dtype)], axis=1)
    lb_cum = jnp.cumsum(jax.nn.softmax(c_lb_logits.astype(F32), axis=0), axis=0)
    lb_all = lb_cum - lb_cum[:1]

    def row(p):
        return p.reshape(1, -1)

    zp = lambda *sh: jnp.zeros((bp,) + sh, F32)
    ys_pad = jnp.pad(x_sample, ((0, 0), (0, SAMPLE_CHUNK - ts), (0, 0))).reshape(bs * SAMPLE_CHUNK, D_MODEL)
    yp = x_prompt.reshape(bp * tp, D_MODEL)
    mem2 = mem_prompt.reshape(bp * N_MEM, D_MODEL)

    outs = [[] for _ in range(12)]
    for l in range(DEPTH):
        params = [row(a_mu[l]), row(a_w0[l]), a_w2[l], row(a_a0[l]), a_a2[l], row(a_kk[l]), row(a_ka[l]),
                  row(a_rk[l]), row(a_gn_g[l]), row(a_gn_b[l]), g2_p[l], row(b_g2_bias[l]),
                  row(b_gn_g[l]), row(lb_all[l]), row(c_gn_g[l]), d_conv_w[l]]
        mk, mv = _rms_matmul(mem2, e_mem_g[l], w_kv[l], (GROUP_W, GROUP_W), f"memkv{l}")
        mk3, mv3 = mk.reshape(bp, N_MEM, GROUP_W), mv.reshape(bp, N_MEM, GROUP_W)

        (proj_p,) = _rms_matmul(yp, ln_pre[l], w_in_p[l], (D_PROJ_P,), f"inproj_p{l}")
        mix_p, p_wkv, p_shift, p_gla, p_hgrn, p_conv = _mixers(
            proj_p.reshape(bp, tp, D_PROJ_P), mk3, mv3,
            zp(N_HEADS, HEAD_DIM, HEAD_DIM), zp(1, A_SHIFT), zp(GROUP_W, B_QK), zp(GROUP_W, C_F),
            zp(D_CONV_W - 1, GROUP_W), params, PROMPT_CHUNK, PROMPT_CHUNK, f"mixers_p{l}")
        yp = _outproj(mix_p.reshape(bp * tp, D_MIX), w_out_b[l], ln_post[l], yp, f"outproj_p{l}")

        (proj_s,) = _rms_matmul(ys_pad, ln_pre[l], w_in_p[l], (D_PROJ_P,), f"inproj_s{l}")
        mix_s, s_wkv, s_shift, s_gla, s_hgrn, s_conv = _mixers(
            proj_s.reshape(bs, SAMPLE_CHUNK, D_PROJ_P),
            cache_mem_k[l].reshape(bs, N_MEM, GROUP_W), cache_mem_v[l].reshape(bs, N_MEM, GROUP_W),
            state_a_wkv[l], state_a_shift[l].reshape(bs, 1, A_SHIFT),
            _to_blockdiag_t(state_b_gla[l]), _to_blockdiag_t(state_c_hgrn[l]), state_d_conv[l],
            params, SAMPLE_CHUNK, ts, f"mixers_s{l}")
        ys_pad = _outproj(mix_s.reshape(bs * SAMPLE_CHUNK, D_MIX), w_out_b[l], ln_post[l], ys_pad,
                          f"outproj_s{l}")

        for lst, val in zip(outs, (
                p_wkv, p_shift.reshape(bp, A_SHIFT), _from_blockdiag_t(p_gla, B_DK),
                _from_blockdiag_t(p_hgrn, C_EXPAND), p_conv,
                mk.reshape(bp, N_MEM, N_HEADS, HEAD_DIM), mv.reshape(bp, N_MEM, N_HEADS, HEAD_DIM),
                s_wkv, s_shift.reshape(bs, A_SHIFT), _from_blockdiag_t(s_gla, B_DK),
                _from_blockdiag_t(s_hgrn, C_EXPAND), s_conv)):
            lst.append(val)

    y_prompt = yp.reshape(bp, tp, D_MODEL)
    y_sample = ys_pad.reshape(bs, SAMPLE_CHUNK, D_MODEL)[:, :ts]
    return (y_prompt, y_sample) + tuple(jnp.stack(o) for o in outs)
```

```python
import functools

import numpy as np
import jax
import jax.numpy as jnp
from jax import lax
from jax.experimental import pallas as pl
from jax.experimental.pallas import tpu as pltpu

F32 = jnp.float32
BF16 = jnp.bfloat16

D_MODEL = 1024
DEPTH = 4
N_MEM = 256
GROUP_W = 256
N_HEADS = 4
HEAD_DIM = 64
A_LORA = 64
A_SHIFT = 3 * GROUP_W + 2 * A_LORA
A_GN_EPS = 64e-5
B_DK = 32
B_QK = N_HEADS * B_DK
B_LORA_G = 16
B_GATE_NORM = 16.0
C_EXPAND = 64
C_F = N_HEADS * C_EXPAND
D_CONV_W = 3
D_MIX = 5 * GROUP_W
EPS = 1e-6
LANES = 128
SUBLANES = 8

_REF_WIDTHS = (A_SHIFT, GROUP_W, B_QK, B_QK, GROUP_W, B_LORA_G, GROUP_W, C_F, C_F, GROUP_W, GROUP_W,
               GROUP_W, GROUP_W, GROUP_W, GROUP_W, GROUP_W, GROUP_W)
_REF_OFF = np.concatenate([[0], np.cumsum(_REF_WIDTHS)]).tolist()
_NAMES = ("a_x", "a_z", "b_q", "b_k", "b_v", "b_gl", "b_z", "c_q", "c_f", "c_i", "c_z",
          "d_b", "d_c", "d_x", "d_z", "e_q", "e_z")
_PACK_ORDER = ("a_x", "a_z", "b_q", "b_k", "b_v", "b_z", "c_q", "c_f", "c_i", "c_z",
               "d_b", "d_c", "d_x", "d_z", "e_q", "e_z", "b_gl")
_OFF = {}
_o = 0
for _n in _PACK_ORDER:
    _w = _REF_WIDTHS[_NAMES.index(_n)]
    _OFF[_n] = (_o, _o + _w)
    _o += -(-_w // LANES) * LANES
D_PROJ_P = _o

PROMPT_CHUNK = 64
SAMPLE_CHUNK = 8
PROMPT_SEQS = 4
SAMPLE_SEQS = 8
ROW_TILE_IN = 256
ROW_TILE_OUT = 512
VMEM_LIMIT = 48 * 1024 * 1024


def _bf(x):
    return x.astype(BF16)


def _mm(a, b):
    return lax.dot_general(_bf(a), _bf(b), (((1,), (0,)), ((), ())), preferred_element_type=F32)


def _mm_nt(a, b):
    return lax.dot_general(_bf(a), _bf(b), (((1,), (1,)), ((), ())), preferred_element_type=F32)


def _mm_tn(a, b):
    return lax.dot_general(_bf(a), _bf(b), (((0,), (0,)), ((), ())), preferred_element_type=F32)


def _rows(shape):
    return lax.broadcasted_iota(jnp.int32, shape, 0)


def _cols(shape):
    return lax.broadcasted_iota(jnp.int32, shape, 1)


def _shift_rows(x, s):
    if s == 0:
        return x
    return jnp.where(_rows(x.shape) >= s, pltpu.roll(x, s, axis=0), 0.0)


def _cumsum_rows(x):
    s = 1
    while s < x.shape[0]:
        x = x + _shift_rows(x, s)
        s *= 2
    return x


def _softplus(x):
    return jnp.maximum(x, 0.0) + jnp.log(1.0 + jnp.exp(-jnp.abs(x)))


def _log_sigmoid(x):
    return -_softplus(-x)


def _silu(x):
    return x * jax.nn.sigmoid(x)


def _head_sum(x, ones_bd):
    hi = _bf(x)
    lo = _bf(x - hi.astype(F32))
    return (jnp.dot(hi, ones_bd, preferred_element_type=F32)
            + jnp.dot(lo, ones_bd, preferred_element_type=F32))


def _log2(n):
    return int(np.log2(n))


def _rms_matmul_kernel(x_ref, g_ref, w_ref, *o_refs):
    x = x_ref[...]
    u = _bf(x * lax.rsqrt(jnp.mean(x * x, axis=-1, keepdims=True) + EPS) * g_ref[...])
    off = 0
    for o_ref in o_refs:
        n = o_ref.shape[-1]
        o_ref[...] = jnp.dot(u, w_ref[:, off:off + n], preferred_element_type=F32)
        off += n


def _rms_matmul(x, g, w, out_widths, name):
    n, d = x.shape
    tm = min(ROW_TILE_IN, n)
    return pl.pallas_call(
        _rms_matmul_kernel,
        grid=(n // tm,),
        in_specs=[pl.BlockSpec((tm, d), lambda i: (i, 0)),
                  pl.BlockSpec((1, d), lambda i: (0, 0)),
                  pl.BlockSpec(w.shape, lambda i: (0, 0))],
        out_specs=[pl.BlockSpec((tm, ow), lambda i: (i, 0)) for ow in out_widths],
        out_shape=[jax.ShapeDtypeStruct((n, ow), F32) for ow in out_widths],
        compiler_params=pltpu.CompilerParams(dimension_semantics=("arbitrary",),
                                             vmem_limit_bytes=VMEM_LIMIT),
        name=name,
    )(x, g.reshape(1, d), w)


def _outproj_kernel(mix_ref, w_ref, g_ref, x_ref, y_ref):
    z = jnp.dot(mix_ref[...], w_ref[...], preferred_element_type=F32)
    y_ref[...] = x_ref[...] + z * lax.rsqrt(jnp.mean(z * z, axis=-1, keepdims=True) + EPS) * g_ref[...]


def _outproj(mix, w, g, x, name):
    n, d = x.shape
    tm = min(ROW_TILE_OUT, n)
    return pl.pallas_call(
        _outproj_kernel,
        grid=(n // tm,),
        in_specs=[pl.BlockSpec((tm, D_MIX), lambda i: (i, 0)),
                  pl.BlockSpec((D_MIX, d), lambda i: (0, 0)),
                  pl.BlockSpec((1, d), lambda i: (0, 0)),
                  pl.BlockSpec((tm, d), lambda i: (i, 0))],
        out_specs=pl.BlockSpec((tm, d), lambda i: (i, 0)),
        out_shape=jax.ShapeDtypeStruct((n, d), F32),
        compiler_params=pltpu.CompilerParams(dimension_semantics=("arbitrary",),
                                             vmem_limit_bytes=VMEM_LIMIT),
        name=name,
    )(mix, w, g.reshape(1, d), x)


def _interleave(tasks):
    results = [None] * len(tasks)
    live = list(range(len(tasks)))
    while live:
        still = []
        for i in live:
            try:
                next(tasks[i])
                still.append(i)
            except StopIteration as done:
                results[i] = done.value
        live = still
    return results


def _rwkv_head(r, v, k, at, rt, bt, kt, bh, kh, e_end_c, s0):
    c = r.shape[0]
    ri, ci = _rows((c, c)), _cols((c, c))
    strict, incl = ci < ri, ci <= ri
    ar = jnp.concatenate([at, rt], axis=0)
    gb = _mm_nt(ar, bt)
    gk = _mm_nt(ar, kt)
    w0 = _mm_nt(ar, s0)
    yield
    u = w0[:c] + _mm(jnp.where(strict, gk[:c], 0.0), v)
    p, n = jnp.where(strict, gb[:c], 0.0), 1
    yield
    while n < c:
        u = u + _mm(p, u)
        n *= 2
        if n < c:
            p = _mm(p, p)
        yield
    o = w0[c:] + _mm(jnp.where(incl, gb[c:], 0.0), u) + _mm(jnp.where(incl, gk[c:], 0.0), v)
    s_new = s0 * e_end_c + _mm_tn(u, bh) + _mm_tn(v, kh)
    return o, s_new


def _gla_chunk(q, k, v, lg, st, dk, ones_bd):
    c, wk = q.shape
    b = _cumsum_rows(lg)
    b_c = b[c - 1:c, :]
    o = _mm_nt(q * jnp.exp(b), st)
    rows = _rows((c, wk))
    yield

    half = c // 2
    att = None
    hj_head = lax.shift_right_logical(_rows((N_HEADS * c, wk)), _log2(c))
    k_head = lax.shift_right_logical(_cols((N_HEADS * c, wk)), _log2(dk))
    kmask = hj_head == k_head
    ai, aj = _rows((c, N_HEADS * c)), jnp.bitwise_and(_cols((c, N_HEADS * c)), c - 1)
    while half >= SUBLANES:
        blk = 2 * half
        nb = c // blk
        mid = jnp.broadcast_to(b.reshape(nb, blk, wk)[:, half - 1:half, :], (nb, blk, wk)).reshape(c, wk)
        second = jnp.bitwise_and(rows, blk - 1) >= half
        ql = jnp.where(second, q * jnp.exp(jnp.where(second, b - mid, 0.0)), 0.0)
        kl = jnp.where(second, 0.0, k * jnp.exp(jnp.where(second, 0.0, mid - b)))
        kbd = jnp.where(kmask, jnp.concatenate([kl] * N_HEADS, axis=0), 0.0)
        lvl = _mm_nt(ql, kbd)
        same = lax.shift_right_logical(ai, _log2(blk)) == lax.shift_right_logical(aj, _log2(blk))
        lvl = jnp.where(same, lvl, 0.0)
        att = lvl if att is None else att + lvl
        half //= 2
        yield
    if att is not None:
        vj_head = lax.shift_right_logical(_rows((N_HEADS * c, GROUP_W)), _log2(c))
        vv_head = lax.shift_right_logical(_cols((N_HEADS * c, GROUP_W)), _log2(HEAD_DIM))
        vbd = jnp.where(vj_head == vv_head, jnp.concatenate([v] * N_HEADS, axis=0), 0.0)
        o = o + _mm(att, vbd)
        yield

    in_blk = jnp.bitwise_and(rows, SUBLANES - 1)
    terms, vds = [q * k], [v]
    for d in range(1, SUBLANES):
        kd, bd = pltpu.roll(k, d, axis=0), pltpu.roll(b, d, axis=0)
        terms.append(jnp.where(in_blk >= d, q * kd * jnp.exp(jnp.minimum(b - bd, 0.0)), 0.0))
        vds.append(pltpu.roll(v, d, axis=0))
    s_all = _mm(jnp.concatenate(terms, axis=0), ones_bd)
    sj_head = lax.shift_right_logical(_rows((GROUP_W, wk)), _log2(HEAD_DIM))
    sk_head = lax.shift_right_logical(_cols((GROUP_W, wk)), _log2(dk))
    st_new = st * jnp.exp(b_c) + jnp.where(sj_head == sk_head, _mm_tn(v, k * jnp.exp(b_c - b)), 0.0)
    yield
    for d in range(SUBLANES):
        o = o + s_all[d * c:(d + 1) * c] * vds[d]
    return o, st_new


def _mem_attention(qe, kbd, vbd):
    sc = _mm_nt(qe, kbd)
    yield
    prs = []
    for h in range(N_HEADS):
        s_h = sc[:, h * N_MEM:(h + 1) * N_MEM]
        e_h = jnp.exp(s_h - jnp.max(s_h, axis=-1, keepdims=True))
        prs.append(e_h / jnp.sum(e_h, axis=-1, keepdims=True))
    oe = _mm(jnp.concatenate(prs, axis=-1), vbd)
    yield
    return oe


def _mixers_kernel(t_valid, nc,
                   proj_ref, mk_ref, mv_ref, wkv0_ref, shift0_ref, gla0_ref, hgrn0_ref, conv0_ref,
                   mu_ref, w0_ref, w2_ref, a0_ref, a2_ref, akk_ref, aka_ref, ark_ref, agg_ref, agb_ref,
                   g2_ref, g2b_ref, bgg_ref, lb_ref, cgg_ref, cw_ref, gsum_ref, hsum_ref,
                   mix_ref, wkv_ref, shift_ref, gla_ref, hgrn_ref, conv_ref,
                   kbd_ref, vbd_ref, glat_ref, hgrnt_ref):
    n_seq, c = proj_ref.shape[0], proj_ref.shape[1]
    step = pl.program_id(1)

    @pl.when(step == 0)
    def _():
        wkv_ref[...] = wkv0_ref[...]
        shift_ref[...] = shift0_ref[...]
        conv_ref[...] = conv0_ref[...]
        glat_ref[...] = jnp.zeros(glat_ref.shape, F32)
        hgrnt_ref[...] = jnp.zeros(hgrnt_ref.shape, F32)
        mh = lax.shift_right_logical(_rows((N_HEADS * N_MEM, GROUP_W)), _log2(N_MEM))
        dh = lax.shift_right_logical(_cols((N_HEADS * N_MEM, GROUP_W)), _log2(HEAD_DIM))
        same = mh == dh
        for g in range(n_seq):
            kbd_ref[g] = _bf(jnp.where(same, jnp.concatenate([mk_ref[g]] * N_HEADS, axis=0), 0.0))
            vbd_ref[g] = _bf(jnp.where(same, jnp.concatenate([mv_ref[g]] * N_HEADS, axis=0), 0.0))
            for h in range(N_HEADS):
                rs = slice(h * HEAD_DIM, (h + 1) * HEAD_DIM)
                glat_ref[g, rs, h * B_DK:(h + 1) * B_DK] = gla0_ref[g, h].T
                hgrnt_ref[g, rs, h * C_EXPAND:(h + 1) * C_EXPAND] = hgrn0_ref[g, h].T

    row1 = _rows((c, 1))
    pad = t_valid < c
    valid = row1 < t_valid
    last = (t_valid if pad else c) - 1

    def prepare(g, st):
        def seg(name):
            lo, hi = _OFF[name]
            return proj_ref[g, :, lo:hi]

        ax = seg("a_x")
        prev = _shift_rows(ax, 1) + jnp.where(row1 == 0, st["shift"], 0.0)
        amix = ax + (prev - ax) * mu_ref[...]
        r = amix[:, 0:GROUP_W]
        k = amix[:, GROUP_W:2 * GROUP_W]
        v = amix[:, 2 * GROUP_W:3 * GROUP_W]
        wl = amix[:, 3 * GROUP_W:3 * GROUP_W + A_LORA]
        al = amix[:, 3 * GROUP_W + A_LORA:A_SHIFT]
        w_raw = -_softplus(-(w0_ref[...] + _mm(jnp.tanh(wl), w2_ref[...]))) - 0.5
        lw = -jnp.exp(w_raw)
        a = jax.nn.sigmoid(a0_ref[...] + _mm(al, a2_ref[...]))
        kk = k * akk_ref[...]
        kk = kk * lax.rsqrt(jnp.maximum(_head_sum(kk * kk, hsum_ref[...]), 1e-24))
        k = k * (1.0 + (a - 1.0) * aka_ref[...])
        bb = kk * a
        if pad:
            lw = jnp.where(valid, lw, 0.0)
            bb = jnp.where(valid, bb, 0.0)
            k = jnp.where(valid, k, 0.0)
        cum = _cumsum_rows(lw)
        cum_c = cum[c - 1:c, :]
        e_cum, e_inv, e_end = jnp.exp(cum), jnp.exp(-cum), jnp.exp(cum_c - cum)
        at = -kk * jnp.exp(cum - lw)
        rt = r * e_cum
        bt, kt = bb * e_inv, k * e_inv
        bh, kh = bb * e_end, k * e_end
        e_end_c = jnp.exp(cum_c)
        tasks = []
        for h in range(N_HEADS):
            sl = slice(h * HEAD_DIM, (h + 1) * HEAD_DIM)
            tasks.append(_rwkv_head(r[:, sl], v[:, sl], k[:, sl], at[:, sl], rt[:, sl], bt[:, sl],
                                    kt[:, sl], bh[:, sl], kh[:, sl], e_end_c[:, sl], st["wkv"][h]))

        bq = seg("b_q") * (B_DK ** -0.5)
        bk = seg("b_k")
        lo, _ = _OFF["b_gl"]
        lg = _log_sigmoid(_mm(proj_ref[g, :, lo:lo + LANES], g2_ref[...]) + g2b_ref[...]) * (1.0 / B_GATE_NORM)
        if pad:
            lg = jnp.where(valid, lg, 0.0)
            bk = jnp.where(valid, bk, 0.0)
        tasks.append(_gla_chunk(bq, bk, seg("b_v"), lg, st["gla"], B_DK, gsum_ref[...]))

        lb = lb_ref[...]
        cq = _silu(seg("c_q")) * (C_EXPAND ** -0.5)
        cf = seg("c_f")
        la, lbb = jnp.log(lb), jnp.log1p(-lb) + _log_sigmoid(cf)
        log_f = jnp.maximum(la, lbb) + jnp.log(1.0 + jnp.exp(-jnp.abs(la - lbb)))
        kc = (1.0 - lb) * jax.nn.sigmoid(-cf)
        if pad:
            log_f = jnp.where(valid, log_f, 0.0)
            kc = jnp.where(valid, kc, 0.0)
        tasks.append(_gla_chunk(cq, kc, seg("c_i"), log_f, st["hgrn"], C_EXPAND, hsum_ref[...]))

        tasks.append(_mem_attention(seg("e_q") * (HEAD_DIM ** -0.5), kbd_ref[g], vbd_ref[g]))

        def finish(res):
            new = {"wkv": [s_h for _, s_h in res[:N_HEADS]], "shift": ax[last:last + 1, :]}
            o_a = jnp.concatenate([o_h for o_h, _ in res[:N_HEADS]], axis=-1)
            mean = _head_sum(o_a, hsum_ref[...]) * (1.0 / HEAD_DIM)
            var = _head_sum(jnp.square(o_a - mean), hsum_ref[...]) * (1.0 / HEAD_DIM)
            o_a = (o_a - mean) * lax.rsqrt(var + A_GN_EPS) * agg_ref[...] + agb_ref[...]
            bonus = _head_sum(r * k * ark_ref[...], hsum_ref[...]) * v
            out_a = (o_a + bonus) * _silu(seg("a_z"))

            o_b, new["gla"] = res[N_HEADS]
            o_b = o_b * lax.rsqrt(_head_sum(o_b * o_b, hsum_ref[...]) * (1.0 / HEAD_DIM) + EPS)
            out_b = o_b * bgg_ref[...] * _silu(seg("b_z"))

            o_c, new["hgrn"] = res[N_HEADS + 1]
            o_c = o_c * lax.rsqrt(jnp.mean(o_c * o_c, axis=-1, keepdims=True) + EPS) * cgg_ref[...]
            out_c = o_c * _silu(seg("c_z"))

            cu = seg("d_c") * seg("d_x")
            p1 = _shift_rows(cu, 1) + jnp.where(row1 == 0, st["conv"][1:2, :], 0.0)
            p2 = (_shift_rows(cu, 2) + jnp.where(row1 == 0, st["conv"][0:1, :], 0.0)
                  + jnp.where(row1 == 1, st["conv"][1:2, :], 0.0))
            conv = p2 * cw_ref[0:1, :] + p1 * cw_ref[1:2, :] + cu * cw_ref[2:3, :]
            out_d = seg("d_b") * conv * _silu(seg("d_z"))
            new["conv"] = cu[last - 1:last + 1, :]

            out_e = res[N_HEADS + 2] * _silu(seg("e_z"))
            new["mix"] = _bf(jnp.concatenate([out_a, out_b, out_c, out_d, out_e], axis=-1))
            return new

        return tasks, finish

    olds = [{"wkv": [wkv_ref[g, h] for h in range(N_HEADS)], "shift": shift_ref[g], "gla": glat_ref[g],
             "hgrn": hgrnt_ref[g], "conv": conv_ref[g]} for g in range(n_seq)]
    prepared = [prepare(g, olds[g]) for g in range(n_seq)]
    n_tasks = N_HEADS + 3
    results = _interleave([t for tasks, _ in prepared for t in tasks])
    for g, (_, finish) in enumerate(prepared):
        new = finish(results[g * n_tasks:(g + 1) * n_tasks])
        for h in range(N_HEADS):
            wkv_ref[g, h] = new["wkv"][h]
        shift_ref[g] = new["shift"]
        glat_ref[g] = new["gla"]
        hgrnt_ref[g] = new["hgrn"]
        conv_ref[g] = new["conv"]
        mix_ref[g] = new["mix"]

    @pl.when(step == nc - 1)
    def _():
        for g in range(n_seq):
            for h in range(N_HEADS):
                rs = slice(h * HEAD_DIM, (h + 1) * HEAD_DIM)
                gla_ref[g, h] = glat_ref[g, rs, h * B_DK:(h + 1) * B_DK].T
                hgrn_ref[g, h] = hgrnt_ref[g, rs, h * C_EXPAND:(h + 1) * C_EXPAND].T


def _mixers(proj, mk, mv, wkv0, shift0, gla0, hgrn0, conv0, params, chunk, t_valid, n_seq, name):
    s, t, _ = proj.shape
    nc = t // chunk
    seq3 = lambda i, j: (i, 0, 0)
    seq4 = lambda i, j: (i, 0, 0, 0)
    par = lambda i, j: (0, 0)
    state_blocks = [((n_seq, N_HEADS, HEAD_DIM, HEAD_DIM), seq4),
                    ((n_seq, 1, A_SHIFT), seq3),
                    ((n_seq, N_HEADS, B_DK, HEAD_DIM), seq4),
                    ((n_seq, N_HEADS, C_EXPAND, HEAD_DIM), seq4),
                    ((n_seq, D_CONV_W - 1, GROUP_W), seq3)]
    in_specs = [pl.BlockSpec((n_seq, chunk, D_PROJ_P), lambda i, j: (i, j, 0)),
                pl.BlockSpec((n_seq, N_MEM, GROUP_W), seq3),
                pl.BlockSpec((n_seq, N_MEM, GROUP_W), seq3)]
    in_specs += [pl.BlockSpec(b, m) for b, m in state_blocks]
    in_specs += [pl.BlockSpec(p.shape, par) for p in params]
    out_specs = [pl.BlockSpec((n_seq, chunk, D_MIX), lambda i, j: (i, j, 0))]
    out_specs += [pl.BlockSpec(b, m) for b, m in state_blocks]
    out_shape = [jax.ShapeDtypeStruct((s, t, D_MIX), BF16)]
    out_shape += [jax.ShapeDtypeStruct((s,) + b[1:], F32) for b, _ in state_blocks]
    return pl.pallas_call(
        functools.partial(_mixers_kernel, t_valid, nc),
        grid=(s // n_seq, nc),
        in_specs=in_specs,
        out_specs=out_specs,
        out_shape=out_shape,
        scratch_shapes=[pltpu.VMEM((n_seq, N_HEADS * N_MEM, GROUP_W), BF16),
                        pltpu.VMEM((n_seq, N_HEADS * N_MEM, GROUP_W), BF16),
                        pltpu.VMEM((n_seq, GROUP_W, B_QK), F32),
                        pltpu.VMEM((n_seq, GROUP_W, C_F), F32)],
        compiler_params=pltpu.CompilerParams(dimension_semantics=("arbitrary", "arbitrary"),
                                             vmem_limit_bytes=VMEM_LIMIT),
        name=name,
    )(proj, mk, mv, wkv0, shift0, gla0, hgrn0, conv0, *params)


def kernel(x_prompt, x_sample, mem_prompt, state_a_wkv, state_a_shift, state_b_gla, state_c_hgrn,
           state_d_conv, cache_mem_k, cache_mem_v, ln_pre, ln_post, w_in, w_out, a_mu, a_w0, a_w2,
           a_a0, a_a2, a_kk, a_ka, a_rk, a_gn_g, a_gn_b, b_g2, b_g2_bias, b_gn_g, c_lb_logits,
           c_gn_g, d_conv_w, e_mem_g, e_wk, e_wv):
    bp, tp, _ = x_prompt.shape
    bs, ts, _ = x_sample.shape
    assert tp % PROMPT_CHUNK == 0 and ts <= SAMPLE_CHUNK and ts >= D_CONV_W - 1
    assert bp % PROMPT_SEQS == 0 and bs % SAMPLE_SEQS == 0

    cols = [w_in[:, :, _REF_OFF[_NAMES.index(n)]:_REF_OFF[_NAMES.index(n) + 1]] for n in _PACK_ORDER]
    cols.append(jnp.zeros((DEPTH, D_MODEL, D_PROJ_P - w_in.shape[-1]), w_in.dtype))
    w_in_p = _bf(jnp.concatenate(cols, axis=-1))
    w_out_b = _bf(w_out)
    w_kv = _bf(jnp.concatenate([e_wk, e_wv], axis=-1))
    g2_p = jnp.concatenate([b_g2, jnp.zeros((DEPTH, LANES - B_LORA_G, B_QK), b_g2.dtype)], axis=1)
    lb_cum = jnp.cumsum(jax.nn.softmax(c_lb_logits.astype(F32), axis=0), axis=0)
    lb_all = lb_cum - lb_cum[:1]

    def row(p):
        return p.reshape(1, -1)

    def head_blocks(rows_per_head):
        rh = np.arange(N_HEADS * rows_per_head)[:, None] // rows_per_head
        return jnp.asarray(rh == np.arange(GROUP_W)[None, :] // HEAD_DIM, dtype=BF16)

    gsum, hsum = head_blocks(B_DK), head_blocks(HEAD_DIM)

    zp = lambda *sh: jnp.zeros((bp,) + sh, F32)
    ys_pad = jnp.pad(x_sample, ((0, 0), (0, SAMPLE_CHUNK - ts), (0, 0))).reshape(bs * SAMPLE_CHUNK, D_MODEL)
    yp = x_prompt.reshape(bp * tp, D_MODEL)
    mem2 = mem_prompt.reshape(bp * N_MEM, D_MODEL)

    outs = [[] for _ in range(12)]
    for l in range(DEPTH):
        params = [row(a_mu[l]), row(a_w0[l]), a_w2[l], row(a_a0[l]), a_a2[l], row(a_kk[l]), row(a_ka[l]),
                  row(a_rk[l]), row(a_gn_g[l]), row(a_gn_b[l]), g2_p[l], row(b_g2_bias[l]),
                  row(b_gn_g[l]), row(lb_all[l]), row(c_gn_g[l]), d_conv_w[l], gsum, hsum]
        mk, mv = _rms_matmul(mem2, e_mem_g[l], w_kv[l], (GROUP_W, GROUP_W), f"memkv{l}")
        mk3, mv3 = mk.reshape(bp, N_MEM, GROUP_W), mv.reshape(bp, N_MEM, GROUP_W)

        (proj_p,) = _rms_matmul(yp, ln_pre[l], w_in_p[l], (D_PROJ_P,), f"inproj_p{l}")
        mix_p, p_wkv, p_shift, p_gla, p_hgrn, p_conv = _mixers(
            proj_p.reshape(bp, tp, D_PROJ_P), mk3, mv3,
            zp(N_HEADS, HEAD_DIM, HEAD_DIM), zp(1, A_SHIFT), zp(N_HEADS, B_DK, HEAD_DIM),
            zp(N_HEADS, C_EXPAND, HEAD_DIM), zp(D_CONV_W - 1, GROUP_W), params,
            PROMPT_CHUNK, PROMPT_CHUNK, PROMPT_SEQS, f"mixers_p{l}")
        yp = _outproj(mix_p.reshape(bp * tp, D_MIX), w_out_b[l], ln_post[l], yp, f"outproj_p{l}")

        (proj_s,) = _rms_matmul(ys_pad, ln_pre[l], w_in_p[l], (D_PROJ_P,), f"inproj_s{l}")
        mix_s, s_wkv, s_shift, s_gla, s_hgrn, s_conv = _mixers(
            proj_s.reshape(bs, SAMPLE_CHUNK, D_PROJ_P),
            cache_mem_k[l].reshape(bs, N_MEM, GROUP_W), cache_mem_v[l].reshape(bs, N_MEM, GROUP_W),
            state_a_wkv[l], state_a_shift[l].reshape(bs, 1, A_SHIFT),
            state_b_gla[l], state_c_hgrn[l], state_d_conv[l],
            params, SAMPLE_CHUNK, ts, SAMPLE_SEQS, f"mixers_s{l}")
        ys_pad = _outproj(mix_s.reshape(bs * SAMPLE_CHUNK, D_MIX), w_out_b[l], ln_post[l], ys_pad,
                          f"outproj_s{l}")

        for lst, val in zip(outs, (
                p_wkv, p_shift.reshape(bp, A_SHIFT), p_gla, p_hgrn, p_conv,
                mk.reshape(bp, N_MEM, N_HEADS, HEAD_DIM), mv.reshape(bp, N_MEM, N_HEADS, HEAD_DIM),
                s_wkv, s_shift.reshape(bs, A_SHIFT), s_gla, s_hgrn, s_conv)):
            lst.append(val)

    y_prompt = yp.reshape(bp, tp, D_MODEL)
    y_sample = ys_pad.reshape(bs, SAMPLE_CHUNK, D_MODEL)[:, :ts]
    return (y_prompt, y_sample) + tuple(jnp.stack(o) for o in outs)
```

```python
import functools

import numpy as np
import jax
import jax.numpy as jnp
from jax import lax
from jax.experimental import pallas as pl
from jax.experimental.pallas import tpu as pltpu

F32 = jnp.float32
BF16 = jnp.bfloat16

D_MODEL = 1024
DEPTH = 4
N_MEM = 256
GROUP_W = 256
N_HEADS = 4
HEAD_DIM = 64
A_LORA = 64
A_SHIFT = 3 * GROUP_W + 2 * A_LORA
A_GN_EPS = 64e-5
B_DK = 32
B_QK = N_HEADS * B_DK
B_LORA_G = 16
B_GATE_NORM = 16.0
C_EXPAND = 64
C_F = N_HEADS * C_EXPAND
D_CONV_W = 3
D_MIX = 5 * GROUP_W
EPS = 1e-6
LANES = 128
SUBLANES = 8

_REF_WIDTHS = (A_SHIFT, GROUP_W, B_QK, B_QK, GROUP_W, B_LORA_G, GROUP_W, C_F, C_F, GROUP_W, GROUP_W,
               GROUP_W, GROUP_W, GROUP_W, GROUP_W, GROUP_W, GROUP_W)
_REF_OFF = np.concatenate([[0], np.cumsum(_REF_WIDTHS)]).tolist()
_NAMES = ("a_x", "a_z", "b_q", "b_k", "b_v", "b_gl", "b_z", "c_q", "c_f", "c_i", "c_z",
          "d_b", "d_c", "d_x", "d_z", "e_q", "e_z")
_PACK_ORDER = ("a_x", "a_z", "b_q", "b_k", "b_v", "b_z", "c_q", "c_f", "c_i", "c_z",
               "d_b", "d_c", "d_x", "d_z", "e_q", "e_z", "b_gl")
_OFF = {}
_o = 0
for _n in _PACK_ORDER:
    _w = _REF_WIDTHS[_NAMES.index(_n)]
    _OFF[_n] = (_o, _o + _w)
    _o += -(-_w // LANES) * LANES
D_PROJ_P = _o

PROMPT_CHUNK = 64
SAMPLE_CHUNK = 8
PROMPT_SEQS = 4
SAMPLE_SEQS = 8
ROW_TILE_IN = 256
ROW_TILE_OUT = 512
VMEM_LIMIT = 48 * 1024 * 1024


def _bf(x):
    return x.astype(BF16)


def _mm(a, b):
    return lax.dot_general(_bf(a), _bf(b), (((1,), (0,)), ((), ())), preferred_element_type=F32)


def _mm_nt(a, b):
    return lax.dot_general(_bf(a), _bf(b), (((1,), (1,)), ((), ())), preferred_element_type=F32)


def _mm_tn(a, b):
    return lax.dot_general(_bf(a), _bf(b), (((0,), (0,)), ((), ())), preferred_element_type=F32)


def _rows(shape):
    return lax.broadcasted_iota(jnp.int32, shape, 0)


def _cols(shape):
    return lax.broadcasted_iota(jnp.int32, shape, 1)


def _shift_rows(x, s):
    if s == 0:
        return x
    return jnp.where(_rows(x.shape) >= s, pltpu.roll(x, s, axis=0), 0.0)


def _cumsum_rows(x):
    s = 1
    while s < x.shape[0]:
        x = x + _shift_rows(x, s)
        s *= 2
    return x


def _softplus(x):
    return jnp.maximum(x, 0.0) + jnp.log(1.0 + jnp.exp(-jnp.abs(x)))


def _log_sigmoid(x):
    return -_softplus(-x)


def _silu(x):
    return x * jax.nn.sigmoid(x)


def _head_sum(x, ones_bd):
    hi = _bf(x)
    lo = _bf(x - hi.astype(F32))
    return (jnp.dot(hi, ones_bd, preferred_element_type=F32)
            + jnp.dot(lo, ones_bd, preferred_element_type=F32))


def _log2(n):
    return int(np.log2(n))


def _rms_matmul_kernel(w_transposed, x_ref, g_ref, w_ref, *o_refs):
    x = x_ref[...]
    u = _bf(x * lax.rsqrt(jnp.mean(x * x, axis=-1, keepdims=True) + EPS) * g_ref[...])
    off = 0
    for o_ref in o_refs:
        n = o_ref.shape[-1]
        if w_transposed:
            o_ref[...] = lax.dot_general(u, w_ref[off:off + n, :], (((1,), (1,)), ((), ())),
                                         preferred_element_type=F32)
        else:
            o_ref[...] = jnp.dot(u, w_ref[:, off:off + n], preferred_element_type=F32)
        off += n


def _rms_matmul(x, g_all, w_all, layer, out_widths, name, w_transposed):
    n, d = x.shape
    tm = min(ROW_TILE_IN, n)
    return pl.pallas_call(
        functools.partial(_rms_matmul_kernel, w_transposed),
        grid=(n // tm,),
        in_specs=[pl.BlockSpec((tm, d), lambda i: (i, 0)),
                  pl.BlockSpec((None, 1, d), lambda i: (layer, 0, 0)),
                  pl.BlockSpec((None,) + w_all.shape[1:], lambda i: (layer, 0, 0))],
        out_specs=[pl.BlockSpec((tm, ow), lambda i: (i, 0)) for ow in out_widths],
        out_shape=[jax.ShapeDtypeStruct((n, ow), F32) for ow in out_widths],
        compiler_params=pltpu.CompilerParams(dimension_semantics=("arbitrary",),
                                             vmem_limit_bytes=VMEM_LIMIT),
        name=name,
    )(x, g_all.reshape(g_all.shape[0], 1, d), w_all)


def _outproj_kernel(mix_ref, w_ref, g_ref, x_ref, y_ref):
    z = jnp.dot(mix_ref[...], w_ref[...], preferred_element_type=F32)
    y_ref[...] = x_ref[...] + z * lax.rsqrt(jnp.mean(z * z, axis=-1, keepdims=True) + EPS) * g_ref[...]


def _outproj(mix, w_all, g_all, layer, x, name):
    n, d = x.shape
    tm = min(ROW_TILE_OUT, n)
    return pl.pallas_call(
        _outproj_kernel,
        grid=(n // tm,),
        in_specs=[pl.BlockSpec((tm, D_MIX), lambda i: (i, 0)),
                  pl.BlockSpec((None, D_MIX, d), lambda i: (layer, 0, 0)),
                  pl.BlockSpec((None, 1, d), lambda i: (layer, 0, 0)),
                  pl.BlockSpec((tm, d), lambda i: (i, 0))],
        out_specs=pl.BlockSpec((tm, d), lambda i: (i, 0)),
        out_shape=jax.ShapeDtypeStruct((n, d), F32),
        compiler_params=pltpu.CompilerParams(dimension_semantics=("arbitrary",),
                                             vmem_limit_bytes=VMEM_LIMIT),
        name=name,
    )(mix, w_all, g_all.reshape(g_all.shape[0], 1, d), x)


def _interleave(tasks):
    results = [None] * len(tasks)
    live = list(range(len(tasks)))
    while live:
        still = []
        for i in live:
            try:
                next(tasks[i])
                still.append(i)
            except StopIteration as done:
                results[i] = done.value
        live = still
    return results


def _rwkv_head(r, v, k, at, rt, bt, kt, bh, kh, e_end_c, s0):
    c = r.shape[0]
    ri, ci = _rows((c, c)), _cols((c, c))
    strict, incl = ci < ri, ci <= ri
    ar = jnp.concatenate([at, rt], axis=0)
    gb = _mm_nt(ar, bt)
    gk = _mm_nt(ar, kt)
    w0 = _mm_nt(ar, s0)
    yield
    u = w0[:c] + _mm(jnp.where(strict, gk[:c], 0.0), v)
    p, n = jnp.where(strict, gb[:c], 0.0), 1
    yield
    while n < c:
        u = u + _mm(p, u)
        n *= 2
        if n < c:
            p = _mm(p, p)
        yield
    o = w0[c:] + _mm(jnp.where(incl, gb[c:], 0.0), u) + _mm(jnp.where(incl, gk[c:], 0.0), v)
    s_new = s0 * e_end_c + _mm_tn(u, bh) + _mm_tn(v, kh)
    return o, s_new


def _gla_chunk(q, k, v, lg, st, dk, ones_bd):
    c, wk = q.shape
    b = _cumsum_rows(lg)
    b_c = b[c - 1:c, :]
    o = _mm_nt(q * jnp.exp(b), st)
    rows = _rows((c, wk))
    sj_head = lax.shift_right_logical(_rows((GROUP_W, wk)), _log2(HEAD_DIM))
    sk_head = lax.shift_right_logical(_cols((GROUP_W, wk)), _log2(dk))
    st_new = st * jnp.exp(b_c) + jnp.where(sj_head == sk_head, _mm_tn(v, k * jnp.exp(b_c - b)), 0.0)

    in_blk = jnp.bitwise_and(rows, SUBLANES - 1)
    terms = [q * k]
    for d in range(1, SUBLANES):
        kd, bd = pltpu.roll(k, d, axis=0), pltpu.roll(b, d, axis=0)
        terms.append(jnp.where(in_blk >= d, q * kd * jnp.exp(jnp.minimum(b - bd, 0.0)), 0.0))
    s_all = _mm(jnp.concatenate(terms, axis=0), ones_bd)
    yield

    if c == SUBLANES:
        for d in range(SUBLANES):
            o = o + s_all[d * c:(d + 1) * c] * (v if d == 0 else pltpu.roll(v, d, axis=0))
        return o, st_new

    assert c == HEAD_DIM, "the attention matrix shares the 64-lane head segments of ones_bd"
    ai, aj = _rows((c, N_HEADS * c)), jnp.bitwise_and(_cols((c, N_HEADS * c)), c - 1)
    lag = ai - aj
    att = jnp.where(lag == 0, s_all[:c], 0.0)
    for d in range(1, SUBLANES):
        att = jnp.where(lag == d, s_all[d * c:(d + 1) * c], att)
    half = c // 2
    hj_head = lax.shift_right_logical(_rows((N_HEADS * c, wk)), _log2(c))
    k_head = lax.shift_right_logical(_cols((N_HEADS * c, wk)), _log2(dk))
    kmask = hj_head == k_head
    while half >= SUBLANES:
        blk = 2 * half
        nb = c // blk
        mid = jnp.broadcast_to(b.reshape(nb, blk, wk)[:, half - 1:half, :], (nb, blk, wk)).reshape(c, wk)
        second = jnp.bitwise_and(rows, blk - 1) >= half
        ql = jnp.where(second, q * jnp.exp(jnp.where(second, b - mid, 0.0)), 0.0)
        kl = jnp.where(second, 0.0, k * jnp.exp(jnp.where(second, 0.0, mid - b)))
        kbd = jnp.where(kmask, jnp.concatenate([kl] * N_HEADS, axis=0), 0.0)
        lvl = _mm_nt(ql, kbd)
        same = lax.shift_right_logical(ai, _log2(blk)) == lax.shift_right_logical(aj, _log2(blk))
        lvl = jnp.where(same, lvl, 0.0)
        att = att + lvl
        half //= 2
        yield
    vj_head = lax.shift_right_logical(_rows((N_HEADS * c, GROUP_W)), _log2(c))
    vv_head = lax.shift_right_logical(_cols((N_HEADS * c, GROUP_W)), _log2(HEAD_DIM))
    vbd = jnp.where(vj_head == vv_head, jnp.concatenate([v] * N_HEADS, axis=0), 0.0)
    o = o + _mm(att, vbd)
    yield
    return o, st_new


def _mem_attention(qe, kbd, vbd):
    sc = _mm(qe, kbd)
    yield
    prs = []
    for h in range(N_HEADS):
        s_h = sc[:, h * N_MEM:(h + 1) * N_MEM]
        e_h = jnp.exp(s_h - jnp.max(s_h, axis=-1, keepdims=True))
        prs.append(e_h / jnp.sum(e_h, axis=-1, keepdims=True))
    oe = _mm_nt(jnp.concatenate(prs, axis=-1), vbd)
    yield
    return oe


def _mixers_kernel(t_valid, nc,
                   proj_ref, mk_ref, mv_ref, wkv0_ref, shift0_ref, gla0_ref, hgrn0_ref, conv0_ref,
                   mu_ref, w0_ref, w2_ref, a0_ref, a2_ref, akk_ref, aka_ref, ark_ref, agg_ref, agb_ref,
                   g2_ref, g2b_ref, bgg_ref, lb_ref, cgg_ref, cw_ref, gsum_ref, hsum_ref,
                   mix_ref, wkv_ref, shift_ref, gla_ref, hgrn_ref, conv_ref,
                   kbd_ref, vbd_ref, glat_ref, hgrnt_ref):
    n_seq, c = proj_ref.shape[0], proj_ref.shape[1]
    step = pl.program_id(1)

    @pl.when(step == 0)
    def _():
        wkv_ref[...] = wkv0_ref[...]
        shift_ref[...] = shift0_ref[...]
        conv_ref[...] = conv0_ref[...]
        glat_ref[...] = jnp.zeros(glat_ref.shape, F32)
        hgrnt_ref[...] = jnp.zeros(hgrnt_ref.shape, F32)
        kbd_ref[...] = jnp.zeros(kbd_ref.shape, BF16)
        vbd_ref[...] = jnp.zeros(vbd_ref.shape, BF16)
        for g in range(n_seq):
            for h in range(N_HEADS):
                rs = slice(h * HEAD_DIM, (h + 1) * HEAD_DIM)
                ms = slice(h * N_MEM, (h + 1) * N_MEM)
                kbd_ref[g, rs, ms] = _bf(mk_ref[g, h])
                vbd_ref[g, rs, ms] = _bf(mv_ref[g, h])
                glat_ref[g, rs, h * B_DK:(h + 1) * B_DK] = gla0_ref[g, h].T
                hgrnt_ref[g, rs, h * C_EXPAND:(h + 1) * C_EXPAND] = hgrn0_ref[g, h].T

    row1 = _rows((c, 1))
    pad = t_valid < c
    valid = row1 < t_valid
    last = (t_valid if pad else c) - 1

    def prepare(g, st):
        def seg(name):
            lo, hi = _OFF[name]
            return proj_ref[g, :, lo:hi]

        ax = seg("a_x")
        prev = _shift_rows(ax, 1) + jnp.where(row1 == 0, st["shift"], 0.0)
        amix = ax + (prev - ax) * mu_ref[...]
        r = amix[:, 0:GROUP_W]
        k = amix[:, GROUP_W:2 * GROUP_W]
        v = amix[:, 2 * GROUP_W:3 * GROUP_W]
        wl = amix[:, 3 * GROUP_W:3 * GROUP_W + A_LORA]
        al = amix[:, 3 * GROUP_W + A_LORA:A_SHIFT]
        w_raw = -_softplus(-(w0_ref[...] + _mm(jnp.tanh(wl), w2_ref[...]))) - 0.5
        lw = -jnp.exp(w_raw)
        a = jax.nn.sigmoid(a0_ref[...] + _mm(al, a2_ref[...]))
        kk = k * akk_ref[...]
        kk = kk * lax.rsqrt(jnp.maximum(_head_sum(kk * kk, hsum_ref[...]), 1e-24))
        k = k * (1.0 + (a - 1.0) * aka_ref[...])
        bb = kk * a
        if pad:
            lw = jnp.where(valid, lw, 0.0)
            bb = jnp.where(valid, bb, 0.0)
            k = jnp.where(valid, k, 0.0)
        cum = _cumsum_rows(lw)
        cum_c = cum[c - 1:c, :]
        e_cum, e_inv, e_end = jnp.exp(cum), jnp.exp(-cum), jnp.exp(cum_c - cum)
        at = -kk * jnp.exp(cum - lw)
        rt = r * e_cum
        bt, kt = bb * e_inv, k * e_inv
        bh, kh = bb * e_end, k * e_end
        e_end_c = jnp.exp(cum_c)
        tasks = []
        for h in range(N_HEADS):
            sl = slice(h * HEAD_DIM, (h + 1) * HEAD_DIM)
            tasks.append(_rwkv_head(r[:, sl], v[:, sl], k[:, sl], at[:, sl], rt[:, sl], bt[:, sl],
                                    kt[:, sl], bh[:, sl], kh[:, sl], e_end_c[:, sl], st["wkv"][h]))

        bq = seg("b_q") * (B_DK ** -0.5)
        bk = seg("b_k")
        lo, _ = _OFF["b_gl"]
        lg = _log_sigmoid(_mm(proj_ref[g, :, lo:lo + LANES], g2_ref[...]) + g2b_ref[...]) * (1.0 / B_GATE_NORM)
        if pad:
            lg = jnp.where(valid, lg, 0.0)
            bk = jnp.where(valid, bk, 0.0)
        tasks.append(_gla_chunk(bq, bk, seg("b_v"), lg, st["gla"], B_DK, gsum_ref[...]))

        lb = lb_ref[...]
        cq = _silu(seg("c_q")) * (C_EXPAND ** -0.5)
        cf = seg("c_f")
        la, lbb = jnp.log(lb), jnp.log1p(-lb) + _log_sigmoid(cf)
        log_f = jnp.maximum(la, lbb) + jnp.log(1.0 + jnp.exp(-jnp.abs(la - lbb)))
        kc = (1.0 - lb) * jax.nn.sigmoid(-cf)
        if pad:
            log_f = jnp.where(valid, log_f, 0.0)
            kc = jnp.where(valid, kc, 0.0)
        tasks.append(_gla_chunk(cq, kc, seg("c_i"), log_f, st["hgrn"], C_EXPAND, hsum_ref[...]))

        tasks.append(_mem_attention(seg("e_q") * (HEAD_DIM ** -0.5), kbd_ref[g], vbd_ref[g]))

        def finish(res):
            new = {"wkv": [s_h for _, s_h in res[:N_HEADS]], "shift": ax[last:last + 1, :]}
            o_a = jnp.concatenate([o_h for o_h, _ in res[:N_HEADS]], axis=-1)
            mean = _head_sum(o_a, hsum_ref[...]) * (1.0 / HEAD_DIM)
            var = _head_sum(jnp.square(o_a - mean), hsum_ref[...]) * (1.0 / HEAD_DIM)
            o_a = (o_a - mean) * lax.rsqrt(var + A_GN_EPS) * agg_ref[...] + agb_ref[...]
            bonus = _head_sum(r * k * ark_ref[...], hsum_ref[...]) * v
            out_a = (o_a + bonus) * _silu(seg("a_z"))

            o_b, new["gla"] = res[N_HEADS]
            o_b = o_b * lax.rsqrt(_head_sum(o_b * o_b, hsum_ref[...]) * (1.0 / HEAD_DIM) + EPS)
            out_b = o_b * bgg_ref[...] * _silu(seg("b_z"))

            o_c, new["hgrn"] = res[N_HEADS + 1]
            o_c = o_c * lax.rsqrt(jnp.mean(o_c * o_c, axis=-1, keepdims=True) + EPS) * cgg_ref[...]
            out_c = o_c * _silu(seg("c_z"))

            cu = seg("d_c") * seg("d_x")
            p1 = _shift_rows(cu, 1) + jnp.where(row1 == 0, st["conv"][1:2, :], 0.0)
            p2 = (_shift_rows(cu, 2) + jnp.where(row1 == 0, st["conv"][0:1, :], 0.0)
                  + jnp.where(row1 == 1, st["conv"][1:2, :], 0.0))
            conv = p2 * cw_ref[0:1, :] + p1 * cw_ref[1:2, :] + cu * cw_ref[2:3, :]
            out_d = seg("d_b") * conv * _silu(seg("d_z"))
            new["conv"] = cu[last - 1:last + 1, :]

            out_e = res[N_HEADS + 2] * _silu(seg("e_z"))
            new["mix"] = _bf(jnp.concatenate([out_a, out_b, out_c, out_d, out_e], axis=-1))
            return new

        return tasks, finish

    olds = [{"wkv": [wkv_ref[g, h] for h in range(N_HEADS)], "shift": shift_ref[g], "gla": glat_ref[g],
             "hgrn": hgrnt_ref[g], "conv": conv_ref[g]} for g in range(n_seq)]
    prepared = [prepare(g, olds[g]) for g in range(n_seq)]
    n_tasks = N_HEADS + 3
    results = _interleave([t for tasks, _ in prepared for t in tasks])
    for g, (_, finish) in enumerate(prepared):
        new = finish(results[g * n_tasks:(g + 1) * n_tasks])
        for h in range(N_HEADS):
            wkv_ref[g, h] = new["wkv"][h]
        shift_ref[g] = new["shift"]
        glat_ref[g] = new["gla"]
        hgrnt_ref[g] = new["hgrn"]
        conv_ref[g] = new["conv"]
        mix_ref[g] = new["mix"]

    @pl.when(step == nc - 1)
    def _():
        for g in range(n_seq):
            for h in range(N_HEADS):
                rs = slice(h * HEAD_DIM, (h + 1) * HEAD_DIM)
                gla_ref[g, h] = glat_ref[g, rs, h * B_DK:(h + 1) * B_DK].T
                hgrn_ref[g, h] = hgrnt_ref[g, rs, h * C_EXPAND:(h + 1) * C_EXPAND].T


def _mixers(proj, mem_kt, mem_vt, mem_layer, wkv0, shift0, gla0, hgrn0, conv0, params, chunk, t_valid,
            n_seq, name):
    s, t, _ = proj.shape
    nc = t // chunk
    seq3 = lambda i, j: (i, 0, 0)
    seq4 = lambda i, j: (i, 0, 0, 0)
    mem5 = lambda i, j: (mem_layer, i, 0, 0, 0)
    par = lambda i, j: (0, 0)
    state_blocks = [((n_seq, N_HEADS, HEAD_DIM, HEAD_DIM), seq4),
                    ((n_seq, 1, A_SHIFT), seq3),
                    ((n_seq, N_HEADS, B_DK, HEAD_DIM), seq4),
                    ((n_seq, N_HEADS, C_EXPAND, HEAD_DIM), seq4),
                    ((n_seq, D_CONV_W - 1, GROUP_W), seq3)]
    in_specs = [pl.BlockSpec((n_seq, chunk, D_PROJ_P), lambda i, j: (i, j, 0)),
                pl.BlockSpec((None, n_seq, N_HEADS, HEAD_DIM, N_MEM), mem5),
                pl.BlockSpec((None, n_seq, N_HEADS, HEAD_DIM, N_MEM), mem5)]
    in_specs += [pl.BlockSpec(b, m) for b, m in state_blocks]
    in_specs += [pl.BlockSpec(p.shape, par) for p in params]
    out_specs = [pl.BlockSpec((n_seq, chunk, D_MIX), lambda i, j: (i, j, 0))]
    out_specs += [pl.BlockSpec(b, m) for b, m in state_blocks]
    out_shape = [jax.ShapeDtypeStruct((s, t, D_MIX), BF16)]
    out_shape += [jax.ShapeDtypeStruct((s,) + b[1:], F32) for b, _ in state_blocks]
    return pl.pallas_call(
        functools.partial(_mixers_kernel, t_valid, nc),
        grid=(s // n_seq, nc),
        in_specs=in_specs,
        out_specs=out_specs,
        out_shape=out_shape,
        scratch_shapes=[pltpu.VMEM((n_seq, GROUP_W, N_HEADS * N_MEM), BF16),
                        pltpu.VMEM((n_seq, GROUP_W, N_HEADS * N_MEM), BF16),
                        pltpu.VMEM((n_seq, GROUP_W, B_QK), F32),
                        pltpu.VMEM((n_seq, GROUP_W, C_F), F32)],
        compiler_params=pltpu.CompilerParams(dimension_semantics=("arbitrary", "arbitrary"),
                                             vmem_limit_bytes=VMEM_LIMIT),
        name=name,
    )(proj, mem_kt, mem_vt, wkv0, shift0, gla0, hgrn0, conv0, *params)


def kernel(x_prompt, x_sample, mem_prompt, state_a_wkv, state_a_shift, state_b_gla, state_c_hgrn,
           state_d_conv, cache_mem_k, cache_mem_v, ln_pre, ln_post, w_in, w_out, a_mu, a_w0, a_w2,
           a_a0, a_a2, a_kk, a_ka, a_rk, a_gn_g, a_gn_b, b_g2, b_g2_bias, b_gn_g, c_lb_logits,
           c_gn_g, d_conv_w, e_mem_g, e_wk, e_wv):
    bp, tp, _ = x_prompt.shape
    bs, ts, _ = x_sample.shape
    assert tp % PROMPT_CHUNK == 0 and ts <= SAMPLE_CHUNK and ts >= D_CONV_W - 1
    assert bp % PROMPT_SEQS == 0 and bs % SAMPLE_SEQS == 0

    w_in_t = jnp.transpose(w_in, (0, 2, 1))
    segs = [w_in_t[:, _REF_OFF[_NAMES.index(n)]:_REF_OFF[_NAMES.index(n) + 1], :] for n in _PACK_ORDER]
    segs.append(jnp.zeros((DEPTH, D_PROJ_P - w_in.shape[-1], D_MODEL), w_in.dtype))
    w_in_pt = _bf(jnp.concatenate(segs, axis=1))
    w_out_b = _bf(w_out)
    cache_kt = jnp.transpose(cache_mem_k, (0, 1, 3, 4, 2))
    cache_vt = jnp.transpose(cache_mem_v, (0, 1, 3, 4, 2))
    w_kv = _bf(jnp.concatenate([e_wk, e_wv], axis=-1))
    g2_p = jnp.concatenate([b_g2, jnp.zeros((DEPTH, LANES - B_LORA_G, B_QK), b_g2.dtype)], axis=1)
    lb_cum = jnp.cumsum(jax.nn.softmax(c_lb_logits.astype(F32), axis=0), axis=0)
    lb_all = lb_cum - lb_cum[:1]

    def row(p):
        return p.reshape(1, -1)

    def head_blocks(rows_per_head):
        rh = np.arange(N_HEADS * rows_per_head)[:, None] // rows_per_head
        return jnp.asarray(rh == np.arange(GROUP_W)[None, :] // HEAD_DIM, dtype=BF16)

    gsum, hsum = head_blocks(B_DK), head_blocks(HEAD_DIM)

    zp = lambda *sh: jnp.zeros((bp,) + sh, F32)
    ys_pad = jnp.pad(x_sample, ((0, 0), (0, SAMPLE_CHUNK - ts), (0, 0))).reshape(bs * SAMPLE_CHUNK, D_MODEL)
    yp = x_prompt.reshape(bp * tp, D_MODEL)
    mem2 = mem_prompt.reshape(bp * N_MEM, D_MODEL)

    outs = [[] for _ in range(12)]
    for l in range(DEPTH):
        params = [row(a_mu[l]), row(a_w0[l]), a_w2[l], row(a_a0[l]), a_a2[l], row(a_kk[l]), row(a_ka[l]),
                  row(a_rk[l]), row(a_gn_g[l]), row(a_gn_b[l]), g2_p[l], row(b_g2_bias[l]),
                  row(b_gn_g[l]), row(lb_all[l]), row(c_gn_g[l]), d_conv_w[l], gsum, hsum]
        mk, mv = _rms_matmul(mem2, e_mem_g, w_kv, l, (GROUP_W, GROUP_W), f"memkv{l}", False)
        mkt = jnp.transpose(mk.reshape(bp, N_MEM, N_HEADS, HEAD_DIM), (0, 2, 3, 1))
        mvt = jnp.transpose(mv.reshape(bp, N_MEM, N_HEADS, HEAD_DIM), (0, 2, 3, 1))

        (proj_p,) = _rms_matmul(yp, ln_pre, w_in_pt, l, (D_PROJ_P,), f"inproj_p{l}", True)
        mix_p, p_wkv, p_shift, p_gla, p_hgrn, p_conv = _mixers(
            proj_p.reshape(bp, tp, D_PROJ_P), mkt[None], mvt[None], 0,
            zp(N_HEADS, HEAD_DIM, HEAD_DIM), zp(1, A_SHIFT), zp(N_HEADS, B_DK, HEAD_DIM),
            zp(N_HEADS, C_EXPAND, HEAD_DIM), zp(D_CONV_W - 1, GROUP_W), params,
            PROMPT_CHUNK, PROMPT_CHUNK, PROMPT_SEQS, f"mixers_p{l}")
        yp = _outproj(mix_p.reshape(bp * tp, D_MIX), w_out_b, ln_post, l, yp, f"outproj_p{l}")

        (proj_s,) = _rms_matmul(ys_pad, ln_pre, w_in_pt, l, (D_PROJ_P,), f"inproj_s{l}", True)
        mix_s, s_wkv, s_shift, s_gla, s_hgrn, s_conv = _mixers(
            proj_s.reshape(bs, SAMPLE_CHUNK, D_PROJ_P), cache_kt, cache_vt, l,
            state_a_wkv[l], state_a_shift[l].reshape(bs, 1, A_SHIFT),
            state_b_gla[l], state_c_hgrn[l], state_d_conv[l],
            params, SAMPLE_CHUNK, ts, SAMPLE_SEQS, f"mixers_s{l}")
        ys_pad = _outproj(mix_s.reshape(bs * SAMPLE_CHUNK, D_MIX), w_out_b, ln_post, l, ys_pad,
                          f"outproj_s{l}")

        for lst, val in zip(outs, (
                p_wkv, p_shift.reshape(bp, A_SHIFT), p_gla, p_hgrn, p_conv, mkt, mvt,
                s_wkv, s_shift.reshape(bs, A_SHIFT), s_gla, s_hgrn, s_conv)):
            lst.append(val)

    y_prompt = yp.reshape(bp, tp, D_MODEL)
    y_sample = ys_pad.reshape(bs, SAMPLE_CHUNK, D_MODEL)[:, :ts]
    stacked = [jnp.stack(o) for o in outs]
    for i in (5, 6):
        stacked[i] = jnp.transpose(stacked[i], (0, 1, 4, 2, 3))
    return (y_prompt, y_sample) + tuple(stacked)
```

```python
import functools

import numpy as np
import jax
import jax.numpy as jnp
from jax import lax
from jax.experimental import pallas as pl
from jax.experimental.pallas import tpu as pltpu

F32 = jnp.float32
BF16 = jnp.bfloat16

D_MODEL = 1024
DEPTH = 4
N_MEM = 256
GROUP_W = 256
N_HEADS = 4
HEAD_DIM = 64
A_LORA = 64
A_SHIFT = 3 * GROUP_W + 2 * A_LORA
A_GN_EPS = 64e-5
B_DK = 32
B_QK = N_HEADS * B_DK
B_LORA_G = 16
B_GATE_NORM = 16.0
C_EXPAND = 64
C_F = N_HEADS * C_EXPAND
D_CONV_W = 3
D_MIX = 5 * GROUP_W
EPS = 1e-6
LANES = 128
SUBLANES = 8

_REF_WIDTHS = (A_SHIFT, GROUP_W, B_QK, B_QK, GROUP_W, B_LORA_G, GROUP_W, C_F, C_F, GROUP_W, GROUP_W,
               GROUP_W, GROUP_W, GROUP_W, GROUP_W, GROUP_W, GROUP_W)
_REF_OFF = np.concatenate([[0], np.cumsum(_REF_WIDTHS)]).tolist()
_NAMES = ("a_x", "a_z", "b_q", "b_k", "b_v", "b_gl", "b_z", "c_q", "c_f", "c_i", "c_z",
          "d_b", "d_c", "d_x", "d_z", "e_q", "e_z")
_PACK_ORDER = ("a_x", "a_z", "b_q", "b_k", "b_v", "b_z", "c_q", "c_f", "c_i", "c_z",
               "d_b", "d_c", "d_x", "d_z", "e_q", "e_z", "b_gl")
_OFF = {}
_o = 0
for _n in _PACK_ORDER:
    _w = _REF_WIDTHS[_NAMES.index(_n)]
    _OFF[_n] = (_o, _o + _w)
    _o += -(-_w // LANES) * LANES
D_PROJ_P = _o

PROMPT_CHUNK = 64
SAMPLE_CHUNK = 8
PROMPT_SEQS = 4
SAMPLE_SEQS = 8
ROW_TILE_IN = 256
PROJ_COLS = 512
ROW_TILE_OUT = 512
VMEM_LIMIT = 48 * 1024 * 1024


def _bf(x):
    return x.astype(BF16)


def _mm(a, b):
    return lax.dot_general(_bf(a), _bf(b), (((1,), (0,)), ((), ())), preferred_element_type=F32)


def _mm_nt(a, b):
    return lax.dot_general(_bf(a), _bf(b), (((1,), (1,)), ((), ())), preferred_element_type=F32)


def _mm_tn(a, b):
    return lax.dot_general(_bf(a), _bf(b), (((0,), (0,)), ((), ())), preferred_element_type=F32)


def _rows(shape):
    return lax.broadcasted_iota(jnp.int32, shape, 0)


def _cols(shape):
    return lax.broadcasted_iota(jnp.int32, shape, 1)


def _shift_rows(x, s):
    if s == 0:
        return x
    return jnp.where(_rows(x.shape) >= s, pltpu.roll(x, s, axis=0), 0.0)


def _cumsum_rows(x):
    s = 1
    while s < x.shape[0]:
        x = x + _shift_rows(x, s)
        s *= 2
    return x


def _softplus(x):
    return jnp.maximum(x, 0.0) + jnp.log(1.0 + jnp.exp(-jnp.abs(x)))


def _log_sigmoid(x):
    return -_softplus(-x)


def _silu(x):
    return x * jax.nn.sigmoid(x)


def _head_sum(x, ones_bd):
    hi = _bf(x)
    lo = _bf(x - hi.astype(F32))
    return (jnp.dot(hi, ones_bd, preferred_element_type=F32)
            + jnp.dot(lo, ones_bd, preferred_element_type=F32))


def _log2(n):
    return int(np.log2(n))


def _rms_matmul_kernel(w_transposed, x_ref, g_ref, w_ref, *o_refs):
    x = x_ref[...]
    u = _bf(x * lax.rsqrt(jnp.mean(x * x, axis=-1, keepdims=True) + EPS) * g_ref[...])
    off = 0
    for o_ref in o_refs:
        n = o_ref.shape[-1]
        if w_transposed:
            o_ref[...] = lax.dot_general(u, w_ref[off:off + n, :], (((1,), (1,)), ((), ())),
                                         preferred_element_type=F32)
        else:
            o_ref[...] = jnp.dot(u, w_ref[:, off:off + n], preferred_element_type=F32)
        off += n


def _rms_matmul(x, g_all, w_all, layer, out_widths, name, w_transposed):
    n, d = x.shape
    tm = min(ROW_TILE_IN, n)
    return pl.pallas_call(
        functools.partial(_rms_matmul_kernel, w_transposed),
        grid=(n // tm,),
        in_specs=[pl.BlockSpec((tm, d), lambda i: (i, 0)),
                  pl.BlockSpec((None, 1, d), lambda i: (layer, 0, 0)),
                  pl.BlockSpec((None,) + w_all.shape[1:], lambda i: (layer, 0, 0))],
        out_specs=[pl.BlockSpec((tm, ow), lambda i: (i, 0)) for ow in out_widths],
        out_shape=[jax.ShapeDtypeStruct((n, ow), F32) for ow in out_widths],
        compiler_params=pltpu.CompilerParams(dimension_semantics=("arbitrary",),
                                             vmem_limit_bytes=VMEM_LIMIT),
        name=name,
    )(x, g_all.reshape(g_all.shape[0], 1, d), w_all)


def _outproj_kernel(mix_ref, w_ref, g_ref, x_ref, y_ref):
    z = jnp.dot(mix_ref[...], w_ref[...], preferred_element_type=F32)
    y_ref[...] = x_ref[...] + z * lax.rsqrt(jnp.mean(z * z, axis=-1, keepdims=True) + EPS) * g_ref[...]


def _outproj(mix, w_all, g_all, layer, x, name):
    n, d = x.shape
    tm = min(ROW_TILE_OUT, n)
    return pl.pallas_call(
        _outproj_kernel,
        grid=(n // tm,),
        in_specs=[pl.BlockSpec((tm, D_MIX), lambda i: (i, 0)),
                  pl.BlockSpec((None, D_MIX, d), lambda i: (layer, 0, 0)),
                  pl.BlockSpec((None, 1, d), lambda i: (layer, 0, 0)),
                  pl.BlockSpec((tm, d), lambda i: (i, 0))],
        out_specs=pl.BlockSpec((tm, d), lambda i: (i, 0)),
        out_shape=jax.ShapeDtypeStruct((n, d), F32),
        compiler_params=pltpu.CompilerParams(dimension_semantics=("arbitrary",),
                                             vmem_limit_bytes=VMEM_LIMIT),
        name=name,
    )(mix, w_all, g_all.reshape(g_all.shape[0], 1, d), x)


def _interleave(tasks):
    results = [None] * len(tasks)
    live = list(range(len(tasks)))
    while live:
        still = []
        for i in live:
            try:
                next(tasks[i])
                still.append(i)
            except StopIteration as done:
                results[i] = done.value
        live = still
    return results


def _rwkv_head(r, v, k, at, rt, bt, kt, bh, kh, e_end_c, s0):
    c = r.shape[0]
    ri, ci = _rows((c, c)), _cols((c, c))
    strict, incl = ci < ri, ci <= ri
    ar = jnp.concatenate([at, rt], axis=0)
    gb = _mm_nt(ar, bt)
    gk = _mm_nt(ar, kt)
    w0 = _mm_nt(ar, s0)
    yield
    u = w0[:c] + _mm(jnp.where(strict, gk[:c], 0.0), v)
    p, n = jnp.where(strict, gb[:c], 0.0), 1
    yield
    while n < c:
        u = u + _mm(p, u)
        n *= 2
        if n < c:
            p = _mm(p, p)
        yield
    o = w0[c:] + _mm(jnp.where(incl, gb[c:], 0.0), u) + _mm(jnp.where(incl, gk[c:], 0.0), v)
    s_new = s0 * e_end_c + _mm_tn(u, bh) + _mm_tn(v, kh)
    return o, s_new


def _gla_chunk(q, k, v, lg, st, dk, ones_bd):
    c, wk = q.shape
    b = _cumsum_rows(lg)
    b_c = b[c - 1:c, :]
    o = _mm_nt(q * jnp.exp(b), st)
    rows = _rows((c, wk))
    sj_head = lax.shift_right_logical(_rows((GROUP_W, wk)), _log2(HEAD_DIM))
    sk_head = lax.shift_right_logical(_cols((GROUP_W, wk)), _log2(dk))
    st_new = st * jnp.exp(b_c) + jnp.where(sj_head == sk_head, _mm_tn(v, k * jnp.exp(b_c - b)), 0.0)

    in_blk = jnp.bitwise_and(rows, SUBLANES - 1)
    terms = [q * k]
    for d in range(1, SUBLANES):
        kd, bd = pltpu.roll(k, d, axis=0), pltpu.roll(b, d, axis=0)
        terms.append(jnp.where(in_blk >= d, q * kd * jnp.exp(jnp.minimum(b - bd, 0.0)), 0.0))
    s_all = _mm(jnp.concatenate(terms, axis=0), ones_bd)
    yield

    if c == SUBLANES:
        for d in range(SUBLANES):
            o = o + s_all[d * c:(d + 1) * c] * (v if d == 0 else pltpu.roll(v, d, axis=0))
        return o, st_new

    assert c == HEAD_DIM, "the attention matrix shares the 64-lane head segments of ones_bd"
    ai, aj = _rows((c, N_HEADS * c)), jnp.bitwise_and(_cols((c, N_HEADS * c)), c - 1)
    lag = ai - aj
    att = jnp.where(lag == 0, s_all[:c], 0.0)
    for d in range(1, SUBLANES):
        att = jnp.where(lag == d, s_all[d * c:(d + 1) * c], att)
    half = c // 2
    hj_head = lax.shift_right_logical(_rows((N_HEADS * c, wk)), _log2(c))
    k_head = lax.shift_right_logical(_cols((N_HEADS * c, wk)), _log2(dk))
    kmask = hj_head == k_head
    while half >= SUBLANES:
        blk = 2 * half
        nb = c // blk
        mid = jnp.broadcast_to(b.reshape(nb, blk, wk)[:, half - 1:half, :], (nb, blk, wk)).reshape(c, wk)
        second = jnp.bitwise_and(rows, blk - 1) >= half
        ql = jnp.where(second, q * jnp.exp(jnp.where(second, b - mid, 0.0)), 0.0)
        kl = jnp.where(second, 0.0, k * jnp.exp(jnp.where(second, 0.0, mid - b)))
        kbd = jnp.where(kmask, jnp.concatenate([kl] * N_HEADS, axis=0), 0.0)
        lvl = _mm_nt(ql, kbd)
        same = lax.shift_right_logical(ai, _log2(blk)) == lax.shift_right_logical(aj, _log2(blk))
        lvl = jnp.where(same, lvl, 0.0)
        att = att + lvl
        half //= 2
        yield
    vj_head = lax.shift_right_logical(_rows((N_HEADS * c, GROUP_W)), _log2(c))
    vv_head = lax.shift_right_logical(_cols((N_HEADS * c, GROUP_W)), _log2(HEAD_DIM))
    vbd = jnp.where(vj_head == vv_head, jnp.concatenate([v] * N_HEADS, axis=0), 0.0)
    o = o + _mm(att, vbd)
    yield
    return o, st_new


def _mem_attention(qe, kbd, vbd):
    sc = _mm(qe, kbd)
    yield
    prs = []
    for h in range(N_HEADS):
        s_h = sc[:, h * N_MEM:(h + 1) * N_MEM]
        e_h = jnp.exp(s_h - jnp.max(s_h, axis=-1, keepdims=True))
        prs.append(e_h / jnp.sum(e_h, axis=-1, keepdims=True))
    oe = _mm_nt(jnp.concatenate(prs, axis=-1), vbd)
    yield
    return oe


def _mixers_kernel(t_valid, nc,
                   x_ref, xn_ref, lnpre_ref, win_ref, mk_ref, mv_ref, wkv0_ref, shift0_ref, gla0_ref, hgrn0_ref, conv0_ref,
                   mu_ref, w0_ref, w2_ref, a0_ref, a2_ref, akk_ref, aka_ref, ark_ref, agg_ref, agb_ref,
                   g2_ref, g2b_ref, bgg_ref, lb_ref, cgg_ref, cw_ref, gsum_ref, hsum_ref,
                   mix_ref, wkv_ref, shift_ref, gla_ref, hgrn_ref, conv_ref,
                   kbd_ref, vbd_ref, glat_ref, hgrnt_ref, proj_a, proj_b):
    n_seq, c = xn_ref.shape[0], xn_ref.shape[1]
    cps = x_ref.shape[1] // c
    step = pl.program_id(1)

    def project(x3, dst):
        x = x3.reshape(n_seq * c, D_MODEL)
        u = _bf(x * lax.rsqrt(jnp.mean(x * x, axis=-1, keepdims=True) + EPS) * lnpre_ref[...])
        for lo in range(0, D_PROJ_P, PROJ_COLS):
            dst[:, lo:lo + PROJ_COLS] = lax.dot_general(
                u, win_ref[lo:lo + PROJ_COLS, :], (((1,), (1,)), ((), ())), preferred_element_type=F32)
            yield

    @pl.when(step == 0)
    def _():
        _interleave([project(x_ref[:, 0:c, :], proj_a)])
        wkv_ref[...] = wkv0_ref[...]
        shift_ref[...] = shift0_ref[...]
        conv_ref[...] = conv0_ref[...]
        glat_ref[...] = jnp.zeros(glat_ref.shape, F32)
        hgrnt_ref[...] = jnp.zeros(hgrnt_ref.shape, F32)
        kbd_ref[...] = jnp.zeros(kbd_ref.shape, BF16)
        vbd_ref[...] = jnp.zeros(vbd_ref.shape, BF16)
        for g in range(n_seq):
            for h in range(N_HEADS):
                rs = slice(h * HEAD_DIM, (h + 1) * HEAD_DIM)
                ms = slice(h * N_MEM, (h + 1) * N_MEM)
                kbd_ref[g, rs, ms] = _bf(mk_ref[g, h])
                vbd_ref[g, rs, ms] = _bf(mv_ref[g, h])
                glat_ref[g, rs, h * B_DK:(h + 1) * B_DK] = gla0_ref[g, h].T
                hgrnt_ref[g, rs, h * C_EXPAND:(h + 1) * C_EXPAND] = hgrn0_ref[g, h].T

    row1 = _rows((c, 1))
    pad = t_valid < c
    valid = row1 < t_valid
    last = (t_valid if pad else c) - 1

    def prepare(proj_ref, g, st):
        def seg(name):
            lo, hi = _OFF[name]
            return proj_ref[g * c:(g + 1) * c, lo:hi]

        ax = seg("a_x")
        prev = _shift_rows(ax, 1) + jnp.where(row1 == 0, st["shift"], 0.0)
        amix = ax + (prev - ax) * mu_ref[...]
        r = amix[:, 0:GROUP_W]
        k = amix[:, GROUP_W:2 * GROUP_W]
        v = amix[:, 2 * GROUP_W:3 * GROUP_W]
        wl = amix[:, 3 * GROUP_W:3 * GROUP_W + A_LORA]
        al = amix[:, 3 * GROUP_W + A_LORA:A_SHIFT]
        w_raw = -_softplus(-(w0_ref[...] + _mm(jnp.tanh(wl), w2_ref[...]))) - 0.5
        lw = -jnp.exp(w_raw)
        a = jax.nn.sigmoid(a0_ref[...] + _mm(al, a2_ref[...]))
        kk = k * akk_ref[...]
        kk = kk * lax.rsqrt(jnp.maximum(_head_sum(kk * kk, hsum_ref[...]), 1e-24))
        k = k * (1.0 + (a - 1.0) * aka_ref[...])
        bb = kk * a
        if pad:
            lw = jnp.where(valid, lw, 0.0)
            bb = jnp.where(valid, bb, 0.0)
            k = jnp.where(valid, k, 0.0)
        cum = _cumsum_rows(lw)
        cum_c = cum[c - 1:c, :]
        e_cum, e_inv, e_end = jnp.exp(cum), jnp.exp(-cum), jnp.exp(cum_c - cum)
        at = -kk * jnp.exp(cum - lw)
        rt = r * e_cum
        bt, kt = bb * e_inv, k * e_inv
        bh, kh = bb * e_end, k * e_end
        e_end_c = jnp.exp(cum_c)
        tasks = []
        for h in range(N_HEADS):
            sl = slice(h * HEAD_DIM, (h + 1) * HEAD_DIM)
            tasks.append(_rwkv_head(r[:, sl], v[:, sl], k[:, sl], at[:, sl], rt[:, sl], bt[:, sl],
                                    kt[:, sl], bh[:, sl], kh[:, sl], e_end_c[:, sl], st["wkv"][h]))

        bq = seg("b_q") * (B_DK ** -0.5)
        bk = seg("b_k")
        lo, _ = _OFF["b_gl"]
        lg = (_log_sigmoid(_mm(proj_ref[g * c:(g + 1) * c, lo:lo + LANES], g2_ref[...]) + g2b_ref[...])
              * (1.0 / B_GATE_NORM))
        if pad:
            lg = jnp.where(valid, lg, 0.0)
            bk = jnp.where(valid, bk, 0.0)
        tasks.append(_gla_chunk(bq, bk, seg("b_v"), lg, st["gla"], B_DK, gsum_ref[...]))

        lb = lb_ref[...]
        cq = _silu(seg("c_q")) * (C_EXPAND ** -0.5)
        cf = seg("c_f")
        la, lbb = jnp.log(lb), jnp.log1p(-lb) + _log_sigmoid(cf)
        log_f = jnp.maximum(la, lbb) + jnp.log(1.0 + jnp.exp(-jnp.abs(la - lbb)))
        kc = (1.0 - lb) * jax.nn.sigmoid(-cf)
        if pad:
            log_f = jnp.where(valid, log_f, 0.0)
            kc = jnp.where(valid, kc, 0.0)
        tasks.append(_gla_chunk(cq, kc, seg("c_i"), log_f, st["hgrn"], C_EXPAND, hsum_ref[...]))

        tasks.append(_mem_attention(seg("e_q") * (HEAD_DIM ** -0.5), kbd_ref[g], vbd_ref[g]))

        def finish(res):
            new = {"wkv": [s_h for _, s_h in res[:N_HEADS]], "shift": ax[last:last + 1, :]}
            o_a = jnp.concatenate([o_h for o_h, _ in res[:N_HEADS]], axis=-1)
            mean = _head_sum(o_a, hsum_ref[...]) * (1.0 / HEAD_DIM)
            var = _head_sum(jnp.square(o_a - mean), hsum_ref[...]) * (1.0 / HEAD_DIM)
            o_a = (o_a - mean) * lax.rsqrt(var + A_GN_EPS) * agg_ref[...] + agb_ref[...]
            bonus = _head_sum(r * k * ark_ref[...], hsum_ref[...]) * v
            out_a = (o_a + bonus) * _silu(seg("a_z"))

            o_b, new["gla"] = res[N_HEADS]
            o_b = o_b * lax.rsqrt(_head_sum(o_b * o_b, hsum_ref[...]) * (1.0 / HEAD_DIM) + EPS)
            out_b = o_b * bgg_ref[...] * _silu(seg("b_z"))

            o_c, new["hgrn"] = res[N_HEADS + 1]
            o_c = o_c * lax.rsqrt(jnp.mean(o_c * o_c, axis=-1, keepdims=True) + EPS) * cgg_ref[...]
            out_c = o_c * _silu(seg("c_z"))

            cu = seg("d_c") * seg("d_x")
            p1 = _shift_rows(cu, 1) + jnp.where(row1 == 0, st["conv"][1:2, :], 0.0)
            p2 = (_shift_rows(cu, 2) + jnp.where(row1 == 0, st["conv"][0:1, :], 0.0)
                  + jnp.where(row1 == 1, st["conv"][1:2, :], 0.0))
            conv = p2 * cw_ref[0:1, :] + p1 * cw_ref[1:2, :] + cu * cw_ref[2:3, :]
            out_d = seg("d_b") * conv * _silu(seg("d_z"))
            new["conv"] = cu[last - 1:last + 1, :]

            out_e = res[N_HEADS + 2] * _silu(seg("e_z"))
            new["mix"] = _bf(jnp.concatenate([out_a, out_b, out_c, out_d, out_e], axis=-1))
            return new

        return tasks, finish

    def run_chunk(proj_ref, olds, side_tasks):
        prepared = [prepare(proj_ref, g, olds[g]) for g in range(n_seq)]
        n_tasks = N_HEADS + 3
        results = _interleave([t for tasks, _ in prepared for t in tasks] + side_tasks)
        return [finish(results[g * n_tasks:(g + 1) * n_tasks]) for g, (_, finish) in enumerate(prepared)]

    states = [{"wkv": [wkv_ref[g, h] for h in range(N_HEADS)], "shift": shift_ref[g], "gla": glat_ref[g],
               "hgrn": hgrnt_ref[g], "conv": conv_ref[g]} for g in range(n_seq)]
    if cps == 1:
        assert nc == 1
        states = run_chunk(proj_a, states, [])
        for g in range(n_seq):
            mix_ref[g] = states[g]["mix"]
    else:
        assert cps == 2
        states = run_chunk(proj_a, states, [project(x_ref[:, c:2 * c, :], proj_b)])
        for g in range(n_seq):
            mix_ref[g, 0:c, :] = states[g]["mix"]
        states = run_chunk(proj_b, states, [project(xn_ref[...], proj_a)])
        for g in range(n_seq):
            mix_ref[g, c:2 * c, :] = states[g]["mix"]
    for g, new in enumerate(states):
        for h in range(N_HEADS):
            wkv_ref[g, h] = new["wkv"][h]
        shift_ref[g] = new["shift"]
        glat_ref[g] = new["gla"]
        hgrnt_ref[g] = new["hgrn"]
        conv_ref[g] = new["conv"]

    @pl.when(step == nc - 1)
    def _():
        for g in range(n_seq):
            for h in range(N_HEADS):
                rs = slice(h * HEAD_DIM, (h + 1) * HEAD_DIM)
                gla_ref[g, h] = glat_ref[g, rs, h * B_DK:(h + 1) * B_DK].T
                hgrn_ref[g, h] = hgrnt_ref[g, rs, h * C_EXPAND:(h + 1) * C_EXPAND].T


def _mixers(x, ln_pre, w_in_pt, layer, mem_kt, mem_vt, mem_layer, wkv0, shift0, gla0, hgrn0, conv0, params,
            chunk, t_valid, n_seq, name):
    s, t, _ = x.shape
    cps = 2 if t // chunk > 1 else 1
    nc = t // (chunk * cps)
    seq3 = lambda i, j: (i, 0, 0)
    seq4 = lambda i, j: (i, 0, 0, 0)
    mem5 = lambda i, j: (mem_layer, i, 0, 0, 0)
    par = lambda i, j: (0, 0)
    state_blocks = [((n_seq, N_HEADS, HEAD_DIM, HEAD_DIM), seq4),
                    ((n_seq, 1, A_SHIFT), seq3),
                    ((n_seq, N_HEADS, B_DK, HEAD_DIM), seq4),
                    ((n_seq, N_HEADS, C_EXPAND, HEAD_DIM), seq4),
                    ((n_seq, D_CONV_W - 1, GROUP_W), seq3)]
    in_specs = [pl.BlockSpec((n_seq, cps * chunk, D_MODEL), lambda i, j: (i, j, 0)),
                pl.BlockSpec((n_seq, chunk, D_MODEL),
                             lambda i, j: (i, jnp.minimum(cps * (j + 1), t // chunk - 1), 0)),
                pl.BlockSpec((None, 1, D_MODEL), lambda i, j: (layer, 0, 0)),
                pl.BlockSpec((None, D_PROJ_P, D_MODEL), lambda i, j: (layer, 0, 0),
                             pipeline_mode=pl.Buffered(1)),
                pl.BlockSpec((None, n_seq, N_HEADS, HEAD_DIM, N_MEM), mem5),
                pl.BlockSpec((None, n_seq, N_HEADS, HEAD_DIM, N_MEM), mem5)]
    in_specs += [pl.BlockSpec(b, m) for b, m in state_blocks]
    in_specs += [pl.BlockSpec(p.shape, par) for p in params]
    out_specs = [pl.BlockSpec((n_seq, cps * chunk, D_MIX), lambda i, j: (i, j, 0))]
    out_specs += [pl.BlockSpec(b, m) for b, m in state_blocks]
    out_shape = [jax.ShapeDtypeStruct((s, t, D_MIX), BF16)]
    out_shape += [jax.ShapeDtypeStruct((s,) + b[1:], F32) for b, _ in state_blocks]
    return pl.pallas_call(
        functools.partial(_mixers_kernel, t_valid, nc),
        grid=(s // n_seq, nc),
        in_specs=in_specs,
        out_specs=out_specs,
        out_shape=out_shape,
        scratch_shapes=[pltpu.VMEM((n_seq, GROUP_W, N_HEADS * N_MEM), BF16),
                        pltpu.VMEM((n_seq, GROUP_W, N_HEADS * N_MEM), BF16),
                        pltpu.VMEM((n_seq, GROUP_W, B_QK), F32),
                        pltpu.VMEM((n_seq, GROUP_W, C_F), F32),
                        pltpu.VMEM((n_seq * chunk, D_PROJ_P), F32),
                        pltpu.VMEM((n_seq * chunk, D_PROJ_P), F32)],
        compiler_params=pltpu.CompilerParams(dimension_semantics=("arbitrary", "arbitrary"),
                                             vmem_limit_bytes=VMEM_LIMIT),
        name=name,
    )(x, x, ln_pre.reshape(ln_pre.shape[0], 1, D_MODEL), w_in_pt, mem_kt, mem_vt, wkv0, shift0, gla0, hgrn0, conv0,
      *params)


def kernel(x_prompt, x_sample, mem_prompt, state_a_wkv, state_a_shift, state_b_gla, state_c_hgrn,
           state_d_conv, cache_mem_k, cache_mem_v, ln_pre, ln_post, w_in, w_out, a_mu, a_w0, a_w2,
           a_a0, a_a2, a_kk, a_ka, a_rk, a_gn_g, a_gn_b, b_g2, b_g2_bias, b_gn_g, c_lb_logits,
           c_gn_g, d_conv_w, e_mem_g, e_wk, e_wv):
    bp, tp, _ = x_prompt.shape
    bs, ts, _ = x_sample.shape
    assert tp % PROMPT_CHUNK == 0 and ts <= SAMPLE_CHUNK and ts >= D_CONV_W - 1
    assert bp % PROMPT_SEQS == 0 and bs % SAMPLE_SEQS == 0

    w_in_t = jnp.transpose(w_in, (0, 2, 1))
    segs = [w_in_t[:, _REF_OFF[_NAMES.index(n)]:_REF_OFF[_NAMES.index(n) + 1], :] for n in _PACK_ORDER]
    segs.append(jnp.zeros((DEPTH, D_PROJ_P - w_in.shape[-1], D_MODEL), w_in.dtype))
    w_in_pt = _bf(jnp.concatenate(segs, axis=1))
    w_out_b = _bf(w_out)
    cache_kt = jnp.transpose(cache_mem_k, (0, 1, 3, 4, 2))
    cache_vt = jnp.transpose(cache_mem_v, (0, 1, 3, 4, 2))
    w_kv = _bf(jnp.concatenate([e_wk, e_wv], axis=-1))
    g2_p = jnp.concatenate([b_g2, jnp.zeros((DEPTH, LANES - B_LORA_G, B_QK), b_g2.dtype)], axis=1)
    lb_cum = jnp.cumsum(jax.nn.softmax(c_lb_logits.astype(F32), axis=0), axis=0)
    lb_all = lb_cum - lb_cum[:1]

    def row(p):
        return p.reshape(1, -1)

    def head_blocks(rows_per_head):
        rh = np.arange(N_HEADS * rows_per_head)[:, None] // rows_per_head
        return jnp.asarray(rh == np.arange(GROUP_W)[None, :] // HEAD_DIM, dtype=BF16)

    gsum, hsum = head_blocks(B_DK), head_blocks(HEAD_DIM)

    zp = lambda *sh: jnp.zeros((bp,) + sh, F32)
    ys_pad = jnp.pad(x_sample, ((0, 0), (0, SAMPLE_CHUNK - ts), (0, 0))).reshape(bs * SAMPLE_CHUNK, D_MODEL)
    yp = x_prompt.reshape(bp * tp, D_MODEL)
    mem2 = mem_prompt.reshape(bp * N_MEM, D_MODEL)

    outs = [[] for _ in range(12)]
    for l in range(DEPTH):
        params = [row(a_mu[l]), row(a_w0[l]), a_w2[l], row(a_a0[l]), a_a2[l], row(a_kk[l]), row(a_ka[l]),
                  row(a_rk[l]), row(a_gn_g[l]), row(a_gn_b[l]), g2_p[l], row(b_g2_bias[l]),
                  row(b_gn_g[l]), row(lb_all[l]), row(c_gn_g[l]), d_conv_w[l], gsum, hsum]
        mk, mv = _rms_matmul(mem2, e_mem_g, w_kv, l, (GROUP_W, GROUP_W), f"memkv{l}", False)
        mkt = jnp.transpose(mk.reshape(bp, N_MEM, N_HEADS, HEAD_DIM), (0, 2, 3, 1))
        mvt = jnp.transpose(mv.reshape(bp, N_MEM, N_HEADS, HEAD_DIM), (0, 2, 3, 1))

        mix_p, p_wkv, p_shift, p_gla, p_hgrn, p_conv = _mixers(
            yp.reshape(bp, tp, D_MODEL), ln_pre, w_in_pt, l, mkt[None], mvt[None], 0,
            zp(N_HEADS, HEAD_DIM, HEAD_DIM), zp(1, A_SHIFT), zp(N_HEADS, B_DK, HEAD_DIM),
            zp(N_HEADS, C_EXPAND, HEAD_DIM), zp(D_CONV_W - 1, GROUP_W), params,
            PROMPT_CHUNK, PROMPT_CHUNK, PROMPT_SEQS, f"mixers_p{l}")
        yp = _outproj(mix_p.reshape(bp * tp, D_MIX), w_out_b, ln_post, l, yp, f"outproj_p{l}")

        mix_s, s_wkv, s_shift, s_gla, s_hgrn, s_conv = _mixers(
            ys_pad.reshape(bs, SAMPLE_CHUNK, D_MODEL), ln_pre, w_in_pt, l, cache_kt, cache_vt, l,
            state_a_wkv[l], state_a_shift[l].reshape(bs, 1, A_SHIFT),
            state_b_gla[l], state_c_hgrn[l], state_d_conv[l],
            params, SAMPLE_CHUNK, ts, SAMPLE_SEQS, f"mixers_s{l}")
        ys_pad = _outproj(mix_s.reshape(bs * SAMPLE_CHUNK, D_MIX), w_out_b, ln_post, l, ys_pad,
                          f"outproj_s{l}")

        for lst, val in zip(outs, (
                p_wkv, p_shift.reshape(bp, A_SHIFT), p_gla, p_hgrn, p_conv, mkt, mvt,
                s_wkv, s_shift.reshape(bs, A_SHIFT), s_gla, s_hgrn, s_conv)):
            lst.append(val)

    y_prompt = yp.reshape(bp, tp, D_MODEL)
    y_sample = ys_pad.reshape(bs, SAMPLE_CHUNK, D_MODEL)[:, :ts]
    stacked = [jnp.stack(o) for o in outs]
    for i in (5, 6):
        stacked[i] = jnp.transpose(stacked[i], (0, 1, 4, 2, 3))
    return (y_prompt, y_sample) + tuple(stacked)
```

```python
import functools

import numpy as np
import jax
import jax.numpy as jnp
from jax import lax
from jax.experimental import pallas as pl
from jax.experimental.pallas import tpu as pltpu

F32 = jnp.float32
BF16 = jnp.bfloat16

D_MODEL = 1024
DEPTH = 4
N_MEM = 256
GROUP_W = 256
N_HEADS = 4
HEAD_DIM = 64
A_LORA = 64
A_SHIFT = 3 * GROUP_W + 2 * A_LORA
A_GN_EPS = 64e-5
B_DK = 32
B_QK = N_HEADS * B_DK
B_LORA_G = 16
B_GATE_NORM = 16.0
C_EXPAND = 64
C_F = N_HEADS * C_EXPAND
D_CONV_W = 3
D_MIX = 5 * GROUP_W
EPS = 1e-6
LANES = 128
SUBLANES = 8

_REF_WIDTHS = (A_SHIFT, GROUP_W, B_QK, B_QK, GROUP_W, B_LORA_G, GROUP_W, C_F, C_F, GROUP_W, GROUP_W,
               GROUP_W, GROUP_W, GROUP_W, GROUP_W, GROUP_W, GROUP_W)
_REF_OFF = np.concatenate([[0], np.cumsum(_REF_WIDTHS)]).tolist()
_NAMES = ("a_x", "a_z", "b_q", "b_k", "b_v", "b_gl", "b_z", "c_q", "c_f", "c_i", "c_z",
          "d_b", "d_c", "d_x", "d_z", "e_q", "e_z")
_PACK_ORDER = ("a_x", "a_z", "b_q", "b_k", "b_v", "b_z", "c_q", "c_f", "c_i", "c_z",
               "d_b", "d_c", "d_x", "d_z", "e_q", "e_z", "b_gl")
_OFF = {}
_o = 0
for _n in _PACK_ORDER:
    _w = _REF_WIDTHS[_NAMES.index(_n)]
    _OFF[_n] = (_o, _o + _w)
    _o += -(-_w // LANES) * LANES
D_PROJ_P = _o

PROMPT_CHUNK = 64
SAMPLE_CHUNK = 8
PROMPT_SEQS = 4
SAMPLE_SEQS = 8
ROW_TILE_IN = 512
ROW_TILE_OUT = 512
VMEM_LIMIT = 48 * 1024 * 1024
PROJ_DTYPE = BF16


def _bf(x):
    return x.astype(BF16)


def _mm(a, b):
    return lax.dot_general(_bf(a), _bf(b), (((1,), (0,)), ((), ())), preferred_element_type=F32)


def _mm_nt(a, b):
    return lax.dot_general(_bf(a), _bf(b), (((1,), (1,)), ((), ())), preferred_element_type=F32)


def _mm_tn(a, b):
    return lax.dot_general(_bf(a), _bf(b), (((0,), (0,)), ((), ())), preferred_element_type=F32)


def _rows(shape):
    return lax.broadcasted_iota(jnp.int32, shape, 0)


def _cols(shape):
    return lax.broadcasted_iota(jnp.int32, shape, 1)


def _shift_rows(x, s):
    if s == 0:
        return x
    return jnp.where(_rows(x.shape) >= s, pltpu.roll(x, s, axis=0), 0.0)


def _cumsum_rows(x):
    s = 1
    while s < x.shape[0]:
        x = x + _shift_rows(x, s)
        s *= 2
    return x


def _softplus(x):
    return jnp.maximum(x, 0.0) + jnp.log(1.0 + jnp.exp(-jnp.abs(x)))


def _log_sigmoid(x):
    return -_softplus(-x)


def _silu(x):
    return x * jax.nn.sigmoid(x)


def _head_sum(x, ones_bd):
    hi = _bf(x)
    lo = _bf(x - hi.astype(F32))
    return (jnp.dot(hi, ones_bd, preferred_element_type=F32)
            + jnp.dot(lo, ones_bd, preferred_element_type=F32))


def _log2(n):
    return int(np.log2(n))


def _rms_matmul_kernel(w_transposed, x_ref, g_ref, w_ref, *o_refs):
    x = x_ref[...]
    u = _bf(x * lax.rsqrt(jnp.mean(x * x, axis=-1, keepdims=True) + EPS) * g_ref[...])
    off = 0
    for o_ref in o_refs:
        n = o_ref.shape[-1]
        if w_transposed:
            y = lax.dot_general(u, w_ref[off:off + n, :], (((1,), (1,)), ((), ())),
                                preferred_element_type=F32)
        else:
            y = jnp.dot(u, w_ref[:, off:off + n], preferred_element_type=F32)
        o_ref[...] = y.astype(o_ref.dtype)
        off += n


def _rms_matmul(x, g_all, w_all, layer, out_widths, name, w_transposed, out_dtype):
    n, d = x.shape
    tm = min(ROW_TILE_IN, n)
    return pl.pallas_call(
        functools.partial(_rms_matmul_kernel, w_transposed),
        grid=(n // tm,),
        in_specs=[pl.BlockSpec((tm, d), lambda i: (i, 0)),
                  pl.BlockSpec((None, 1, d), lambda i: (layer, 0, 0)),
                  pl.BlockSpec((None,) + w_all.shape[1:], lambda i: (layer, 0, 0),
                               pipeline_mode=pl.Buffered(1))],
        out_specs=[pl.BlockSpec((tm, ow), lambda i: (i, 0)) for ow in out_widths],
        out_shape=[jax.ShapeDtypeStruct((n, ow), out_dtype) for ow in out_widths],
        compiler_params=pltpu.CompilerParams(dimension_semantics=("arbitrary",),
                                             vmem_limit_bytes=VMEM_LIMIT),
        name=name,
    )(x, g_all.reshape(g_all.shape[0], 1, d), w_all)


def _outproj_kernel(mix_ref, w_ref, g_ref, x_ref, y_ref):
    z = jnp.dot(mix_ref[...], w_ref[...], preferred_element_type=F32)
    y_ref[...] = x_ref[...] + z * lax.rsqrt(jnp.mean(z * z, axis=-1, keepdims=True) + EPS) * g_ref[...]


def _outproj(mix, w_all, g_all, layer, x, name):
    n, d = x.shape
    tm = min(ROW_TILE_OUT, n)
    return pl.pallas_call(
        _outproj_kernel,
        grid=(n // tm,),
        in_specs=[pl.BlockSpec((tm, D_MIX), lambda i: (i, 0)),
                  pl.BlockSpec((None, D_MIX, d), lambda i: (layer, 0, 0)),
                  pl.BlockSpec((None, 1, d), lambda i: (layer, 0, 0)),
                  pl.BlockSpec((tm, d), lambda i: (i, 0))],
        out_specs=pl.BlockSpec((tm, d), lambda i: (i, 0)),
        out_shape=jax.ShapeDtypeStruct((n, d), F32),
        compiler_params=pltpu.CompilerParams(dimension_semantics=("arbitrary",),
                                             vmem_limit_bytes=VMEM_LIMIT),
        name=name,
    )(mix, w_all, g_all.reshape(g_all.shape[0], 1, d), x)


def _interleave(tasks):
    results = [None] * len(tasks)
    live = list(range(len(tasks)))
    while live:
        still = []
        for i in live:
            try:
                next(tasks[i])
                still.append(i)
            except StopIteration as done:
                results[i] = done.value
        live = still
    return results


def _rwkv_head(r, v, k, at, rt, bt, kt, bh, kh, e_end_c, s0):
    c = r.shape[0]
    ri, ci = _rows((c, c)), _cols((c, c))
    strict, incl = ci < ri, ci <= ri
    ar = jnp.concatenate([at, rt], axis=0)
    gb = _mm_nt(ar, bt)
    gk = _mm_nt(ar, kt)
    w0 = _mm_nt(ar, s0)
    yield
    u = w0[:c] + _mm(jnp.where(strict, gk[:c], 0.0), v)
    p, n = jnp.where(strict, gb[:c], 0.0), 1
    yield
    while n < c:
        u = u + _mm(p, u)
        n *= 2
        if n < c:
            p = _mm(p, p)
        yield
    o = w0[c:] + _mm(jnp.where(incl, gb[c:], 0.0), u) + _mm(jnp.where(incl, gk[c:], 0.0), v)
    s_new = s0 * e_end_c + _mm_tn(u, bh) + _mm_tn(v, kh)
    return o, s_new


def _gla_chunk(q, k, v, lg, st, dk, ones_bd):
    c, wk = q.shape
    b = _cumsum_rows(lg)
    b_c = b[c - 1:c, :]
    o = _mm_nt(q * jnp.exp(b), st)
    rows = _rows((c, wk))
    sj_head = lax.shift_right_logical(_rows((GROUP_W, wk)), _log2(HEAD_DIM))
    sk_head = lax.shift_right_logical(_cols((GROUP_W, wk)), _log2(dk))
    st_new = st * jnp.exp(b_c) + jnp.where(sj_head == sk_head, _mm_tn(v, k * jnp.exp(b_c - b)), 0.0)

    in_blk = jnp.bitwise_and(rows, SUBLANES - 1)
    terms = [q * k]
    for d in range(1, SUBLANES):
        kd, bd = pltpu.roll(k, d, axis=0), pltpu.roll(b, d, axis=0)
        terms.append(jnp.where(in_blk >= d, q * kd * jnp.exp(jnp.minimum(b - bd, 0.0)), 0.0))
    s_all = _mm(jnp.concatenate(terms, axis=0), ones_bd)
    yield

    if c == SUBLANES:
        for d in range(SUBLANES):
            o = o + s_all[d * c:(d + 1) * c] * (v if d == 0 else pltpu.roll(v, d, axis=0))
        return o, st_new

    assert c == HEAD_DIM, "the attention matrix shares the 64-lane head segments of ones_bd"
    ai, aj = _rows((c, N_HEADS * c)), jnp.bitwise_and(_cols((c, N_HEADS * c)), c - 1)
    lag = ai - aj
    att = jnp.where(lag == 0, s_all[:c], 0.0)
    for d in range(1, SUBLANES):
        att = jnp.where(lag == d, s_all[d * c:(d + 1) * c], att)
    half = c // 2
    hj_head = lax.shift_right_logical(_rows((N_HEADS * c, wk)), _log2(c))
    k_head = lax.shift_right_logical(_cols((N_HEADS * c, wk)), _log2(dk))
    kmask = hj_head == k_head
    while half >= SUBLANES:
        blk = 2 * half
        nb = c // blk
        mid = jnp.broadcast_to(b.reshape(nb, blk, wk)[:, half - 1:half, :], (nb, blk, wk)).reshape(c, wk)
        second = jnp.bitwise_and(rows, blk - 1) >= half
        ql = jnp.where(second, q * jnp.exp(jnp.where(second, b - mid, 0.0)), 0.0)
        kl = jnp.where(second, 0.0, k * jnp.exp(jnp.where(second, 0.0, mid - b)))
        kbd = jnp.where(kmask, jnp.concatenate([kl] * N_HEADS, axis=0), 0.0)
        lvl = _mm_nt(ql, kbd)
        same = lax.shift_right_logical(ai, _log2(blk)) == lax.shift_right_logical(aj, _log2(blk))
        lvl = jnp.where(same, lvl, 0.0)
        att = att + lvl
        half //= 2
        yield
    vj_head = lax.shift_right_logical(_rows((N_HEADS * c, GROUP_W)), _log2(c))
    vv_head = lax.shift_right_logical(_cols((N_HEADS * c, GROUP_W)), _log2(HEAD_DIM))
    vbd = jnp.where(vj_head == vv_head, jnp.concatenate([v] * N_HEADS, axis=0), 0.0)
    o = o + _mm(att, vbd)
    yield
    return o, st_new


def _mem_attention(qe, kbd, vbd):
    sc = _mm(qe, kbd)
    yield
    prs = []
    for h in range(N_HEADS):
        s_h = sc[:, h * N_MEM:(h + 1) * N_MEM]
        e_h = jnp.exp(s_h - jnp.max(s_h, axis=-1, keepdims=True))
        prs.append(e_h / jnp.sum(e_h, axis=-1, keepdims=True))
    oe = _mm_nt(jnp.concatenate(prs, axis=-1), vbd)
    yield
    return oe


def _mixers_kernel(t_valid, nc,
                   proj_ref, mk_ref, mv_ref, wkv0_ref, shift0_ref, gla0_ref, hgrn0_ref, conv0_ref,
                   mu_ref, w0_ref, w2_ref, a0_ref, a2_ref, akk_ref, aka_ref, ark_ref, agg_ref, agb_ref,
                   g2_ref, g2b_ref, bgg_ref, lb_ref, cgg_ref, cw_ref, gsum_ref, hsum_ref,
                   mix_ref, wkv_ref, shift_ref, gla_ref, hgrn_ref, conv_ref,
                   kbd_ref, vbd_ref, glat_ref, hgrnt_ref):
    n_seq, c = proj_ref.shape[0], proj_ref.shape[1]
    step = pl.program_id(1)

    @pl.when(step == 0)
    def _():
        wkv_ref[...] = wkv0_ref[...]
        shift_ref[...] = shift0_ref[...]
        conv_ref[...] = conv0_ref[...]
        glat_ref[...] = jnp.zeros(glat_ref.shape, F32)
        hgrnt_ref[...] = jnp.zeros(hgrnt_ref.shape, F32)
        kbd_ref[...] = jnp.zeros(kbd_ref.shape, BF16)
        vbd_ref[...] = jnp.zeros(vbd_ref.shape, BF16)
        for g in range(n_seq):
            for h in range(N_HEADS):
                rs = slice(h * HEAD_DIM, (h + 1) * HEAD_DIM)
                ms = slice(h * N_MEM, (h + 1) * N_MEM)
                kbd_ref[g, rs, ms] = _bf(mk_ref[g, h])
                vbd_ref[g, rs, ms] = _bf(mv_ref[g, h])
                glat_ref[g, rs, h * B_DK:(h + 1) * B_DK] = gla0_ref[g, h].T
                hgrnt_ref[g, rs, h * C_EXPAND:(h + 1) * C_EXPAND] = hgrn0_ref[g, h].T

    row1 = _rows((c, 1))
    pad = t_valid < c
    valid = row1 < t_valid
    last = (t_valid if pad else c) - 1

    def prepare(g, st):
        def seg(name):
            lo, hi = _OFF[name]
            return proj_ref[g, :, lo:hi].astype(F32)

        ax = seg("a_x")
        prev = _shift_rows(ax, 1) + jnp.where(row1 == 0, st["shift"], 0.0)
        amix = ax + (prev - ax) * mu_ref[...]
        r = amix[:, 0:GROUP_W]
        k = amix[:, GROUP_W:2 * GROUP_W]
        v = amix[:, 2 * GROUP_W:3 * GROUP_W]
        wl = amix[:, 3 * GROUP_W:3 * GROUP_W + A_LORA]
        al = amix[:, 3 * GROUP_W + A_LORA:A_SHIFT]
        w_raw = -_softplus(-(w0_ref[...] + _mm(jnp.tanh(wl), w2_ref[...]))) - 0.5
        lw = -jnp.exp(w_raw)
        a = jax.nn.sigmoid(a0_ref[...] + _mm(al, a2_ref[...]))
        kk = k * akk_ref[...]
        kk = kk * lax.rsqrt(jnp.maximum(_head_sum(kk * kk, hsum_ref[...]), 1e-24))
        k = k * (1.0 + (a - 1.0) * aka_ref[...])
        bb = kk * a
        if pad:
            lw = jnp.where(valid, lw, 0.0)
            bb = jnp.where(valid, bb, 0.0)
            k = jnp.where(valid, k, 0.0)
        cum = _cumsum_rows(lw)
        cum_c = cum[c - 1:c, :]
        e_cum, e_inv, e_end = jnp.exp(cum), jnp.exp(-cum), jnp.exp(cum_c - cum)
        at = -kk * jnp.exp(cum - lw)
        rt = r * e_cum
        bt, kt = bb * e_inv, k * e_inv
        bh, kh = bb * e_end, k * e_end
        e_end_c = jnp.exp(cum_c)
        tasks = []
        for h in range(N_HEADS):
            sl = slice(h * HEAD_DIM, (h + 1) * HEAD_DIM)
            tasks.append(_rwkv_head(r[:, sl], v[:, sl], k[:, sl], at[:, sl], rt[:, sl], bt[:, sl],
                                    kt[:, sl], bh[:, sl], kh[:, sl], e_end_c[:, sl], st["wkv"][h]))

        bq = seg("b_q") * (B_DK ** -0.5)
        bk = seg("b_k")
        lo, _ = _OFF["b_gl"]
        lg = (_log_sigmoid(_mm(proj_ref[g, :, lo:lo + LANES].astype(F32), g2_ref[...]) + g2b_ref[...])
              * (1.0 / B_GATE_NORM))
        if pad:
            lg = jnp.where(valid, lg, 0.0)
            bk = jnp.where(valid, bk, 0.0)
        tasks.append(_gla_chunk(bq, bk, seg("b_v"), lg, st["gla"], B_DK, gsum_ref[...]))

        lb = lb_ref[...]
        cq = _silu(seg("c_q")) * (C_EXPAND ** -0.5)
        cf = seg("c_f")
        la, lbb = jnp.log(lb), jnp.log1p(-lb) + _log_sigmoid(cf)
        log_f = jnp.maximum(la, lbb) + jnp.log(1.0 + jnp.exp(-jnp.abs(la - lbb)))
        kc = (1.0 - lb) * jax.nn.sigmoid(-cf)
        if pad:
            log_f = jnp.where(valid, log_f, 0.0)
            kc = jnp.where(valid, kc, 0.0)
        tasks.append(_gla_chunk(cq, kc, seg("c_i"), log_f, st["hgrn"], C_EXPAND, hsum_ref[...]))

        tasks.append(_mem_attention(seg("e_q") * (HEAD_DIM ** -0.5), kbd_ref[g], vbd_ref[g]))

        def finish(res):
            new = {"wkv": [s_h for _, s_h in res[:N_HEADS]], "shift": ax[last:last + 1, :]}
            o_a = jnp.concatenate([o_h for o_h, _ in res[:N_HEADS]], axis=-1)
            mean = _head_sum(o_a, hsum_ref[...]) * (1.0 / HEAD_DIM)
            var = _head_sum(jnp.square(o_a - mean), hsum_ref[...]) * (1.0 / HEAD_DIM)
            o_a = (o_a - mean) * lax.rsqrt(var + A_GN_EPS) * agg_ref[...] + agb_ref[...]
            bonus = _head_sum(r * k * ark_ref[...], hsum_ref[...]) * v
            out_a = (o_a + bonus) * _silu(seg("a_z"))

            o_b, new["gla"] = res[N_HEADS]
            o_b = o_b * lax.rsqrt(_head_sum(o_b * o_b, hsum_ref[...]) * (1.0 / HEAD_DIM) + EPS)
            out_b = o_b * bgg_ref[...] * _silu(seg("b_z"))

            o_c, new["hgrn"] = res[N_HEADS + 1]
            o_c = o_c * lax.rsqrt(jnp.mean(o_c * o_c, axis=-1, keepdims=True) + EPS) * cgg_ref[...]
            out_c = o_c * _silu(seg("c_z"))

            cu = seg("d_c") * seg("d_x")
            p1 = _shift_rows(cu, 1) + jnp.where(row1 == 0, st["conv"][1:2, :], 0.0)
            p2 = (_shift_rows(cu, 2) + jnp.where(row1 == 0, st["conv"][0:1, :], 0.0)
                  + jnp.where(row1 == 1, st["conv"][1:2, :], 0.0))
            conv = p2 * cw_ref[0:1, :] + p1 * cw_ref[1:2, :] + cu * cw_ref[2:3, :]
            out_d = seg("d_b") * conv * _silu(seg("d_z"))
            new["conv"] = cu[last - 1:last + 1, :]

            out_e = res[N_HEADS + 2] * _silu(seg("e_z"))
            new["mix"] = _bf(jnp.concatenate([out_a, out_b, out_c, out_d, out_e], axis=-1))
            return new

        return tasks, finish

    olds = [{"wkv": [wkv_ref[g, h] for h in range(N_HEADS)], "shift": shift_ref[g], "gla": glat_ref[g],
             "hgrn": hgrnt_ref[g], "conv": conv_ref[g]} for g in range(n_seq)]
    prepared = [prepare(g, olds[g]) for g in range(n_seq)]
    n_tasks = N_HEADS + 3
    results = _interleave([t for tasks, _ in prepared for t in tasks])
    for g, (_, finish) in enumerate(prepared):
        new = finish(results[g * n_tasks:(g + 1) * n_tasks])
        for h in range(N_HEADS):
            wkv_ref[g, h] = new["wkv"][h]
        shift_ref[g] = new["shift"]
        glat_ref[g] = new["gla"]
        hgrnt_ref[g] = new["hgrn"]
        conv_ref[g] = new["conv"]
        mix_ref[g] = new["mix"]

    @pl.when(step == nc - 1)
    def _():
        for g in range(n_seq):
            for h in range(N_HEADS):
                rs = slice(h * HEAD_DIM, (h + 1) * HEAD_DIM)
                gla_ref[g, h] = glat_ref[g, rs, h * B_DK:(h + 1) * B_DK].T
                hgrn_ref[g, h] = hgrnt_ref[g, rs, h * C_EXPAND:(h + 1) * C_EXPAND].T


def _mixers(proj, mem_kt, mem_vt, mem_layer, wkv0, shift0, gla0, hgrn0, conv0, params, chunk, t_valid,
            n_seq, name):
    s, t, _ = proj.shape
    nc = t // chunk
    seq3 = lambda i, j: (i, 0, 0)
    seq4 = lambda i, j: (i, 0, 0, 0)
    mem5 = lambda i, j: (mem_layer, i, 0, 0, 0)
    par = lambda i, j: (0, 0)
    state_blocks = [((n_seq, N_HEADS, HEAD_DIM, HEAD_DIM), seq4),
                    ((n_seq, 1, A_SHIFT), seq3),
                    ((n_seq, N_HEADS, B_DK, HEAD_DIM), seq4),
                    ((n_seq, N_HEADS, C_EXPAND, HEAD_DIM), seq4),
                    ((n_seq, D_CONV_W - 1, GROUP_W), seq3)]
    in_specs = [pl.BlockSpec((n_seq, chunk, D_PROJ_P), lambda i, j: (i, j, 0)),
                pl.BlockSpec((None, n_seq, N_HEADS, HEAD_DIM, N_MEM), mem5),
                pl.BlockSpec((None, n_seq, N_HEADS, HEAD_DIM, N_MEM), mem5)]
    in_specs += [pl.BlockSpec(b, m) for b, m in state_blocks]
    in_specs += [pl.BlockSpec(p.shape, par) for p in params]
    out_specs = [pl.BlockSpec((n_seq, chunk, D_MIX), lambda i, j: (i, j, 0))]
    out_specs += [pl.BlockSpec(b, m) for b, m in state_blocks]
    out_shape = [jax.ShapeDtypeStruct((s, t, D_MIX), BF16)]
    out_shape += [jax.ShapeDtypeStruct((s,) + b[1:], F32) for b, _ in state_blocks]
    return pl.pallas_call(
        functools.partial(_mixers_kernel, t_valid, nc),
        grid=(s // n_seq, nc),
        in_specs=in_specs,
        out_specs=out_specs,
        out_shape=out_shape,
        scratch_shapes=[pltpu.VMEM((n_seq, GROUP_W, N_HEADS * N_MEM), BF16),
                        pltpu.VMEM((n_seq, GROUP_W, N_HEADS * N_MEM), BF16),
                        pltpu.VMEM((n_seq, GROUP_W, B_QK), F32),
                        pltpu.VMEM((n_seq, GROUP_W, C_F), F32)],
        compiler_params=pltpu.CompilerParams(dimension_semantics=("arbitrary", "arbitrary"),
                                             vmem_limit_bytes=VMEM_LIMIT),
        name=name,
    )(proj, mem_kt, mem_vt, wkv0, shift0, gla0, hgrn0, conv0, *params)


def kernel(x_prompt, x_sample, mem_prompt, state_a_wkv, state_a_shift, state_b_gla, state_c_hgrn,
           state_d_conv, cache_mem_k, cache_mem_v, ln_pre, ln_post, w_in, w_out, a_mu, a_w0, a_w2,
           a_a0, a_a2, a_kk, a_ka, a_rk, a_gn_g, a_gn_b, b_g2, b_g2_bias, b_gn_g, c_lb_logits,
           c_gn_g, d_conv_w, e_mem_g, e_wk, e_wv):
    bp, tp, _ = x_prompt.shape
    bs, ts, _ = x_sample.shape
    assert tp % PROMPT_CHUNK == 0 and ts <= SAMPLE_CHUNK and ts >= D_CONV_W - 1
    assert bp % PROMPT_SEQS == 0 and bs % SAMPLE_SEQS == 0

    w_in_t = jnp.transpose(w_in, (0, 2, 1))
    segs = [w_in_t[:, _REF_OFF[_NAMES.index(n)]:_REF_OFF[_NAMES.index(n) + 1], :] for n in _PACK_ORDER]
    segs.append(jnp.zeros((DEPTH, D_PROJ_P - w_in.shape[-1], D_MODEL), w_in.dtype))
    w_in_pt = _bf(jnp.concatenate(segs, axis=1))
    w_out_b = _bf(w_out)
    cache_kt = jnp.transpose(cache_mem_k, (0, 1, 3, 4, 2))
    cache_vt = jnp.transpose(cache_mem_v, (0, 1, 3, 4, 2))
    w_kv = _bf(jnp.concatenate([e_wk, e_wv], axis=-1))
    g2_p = jnp.concatenate([b_g2, jnp.zeros((DEPTH, LANES - B_LORA_G, B_QK), b_g2.dtype)], axis=1)
    lb_cum = jnp.cumsum(jax.nn.softmax(c_lb_logits.astype(F32), axis=0), axis=0)
    lb_all = lb_cum - lb_cum[:1]

    def row(p):
        return p.reshape(1, -1)

    def head_blocks(rows_per_head):
        rh = np.arange(N_HEADS * rows_per_head)[:, None] // rows_per_head
        return jnp.asarray(rh == np.arange(GROUP_W)[None, :] // HEAD_DIM, dtype=BF16)

    gsum, hsum = head_blocks(B_DK), head_blocks(HEAD_DIM)

    zp = lambda *sh: jnp.zeros((bp,) + sh, F32)
    ys_pad = jnp.pad(x_sample, ((0, 0), (0, SAMPLE_CHUNK - ts), (0, 0))).reshape(bs * SAMPLE_CHUNK, D_MODEL)
    yp = x_prompt.reshape(bp * tp, D_MODEL)
    mem2 = mem_prompt.reshape(bp * N_MEM, D_MODEL)

    outs = [[] for _ in range(12)]
    for l in range(DEPTH):
        params = [row(a_mu[l]), row(a_w0[l]), a_w2[l], row(a_a0[l]), a_a2[l], row(a_kk[l]), row(a_ka[l]),
                  row(a_rk[l]), row(a_gn_g[l]), row(a_gn_b[l]), g2_p[l], row(b_g2_bias[l]),
                  row(b_gn_g[l]), row(lb_all[l]), row(c_gn_g[l]), d_conv_w[l], gsum, hsum]
        mk, mv = _rms_matmul(mem2, e_mem_g, w_kv, l, (GROUP_W, GROUP_W), f"memkv{l}", False, F32)
        mkt = jnp.transpose(mk.reshape(bp, N_MEM, N_HEADS, HEAD_DIM), (0, 2, 3, 1))
        mvt = jnp.transpose(mv.reshape(bp, N_MEM, N_HEADS, HEAD_DIM), (0, 2, 3, 1))

        (proj_p,) = _rms_matmul(yp, ln_pre, w_in_pt, l, (D_PROJ_P,), f"inproj_p{l}", True, PROJ_DTYPE)
        mix_p, p_wkv, p_shift, p_gla, p_hgrn, p_conv = _mixers(
            proj_p.reshape(bp, tp, D_PROJ_P), mkt[None], mvt[None], 0,
            zp(N_HEADS, HEAD_DIM, HEAD_DIM), zp(1, A_SHIFT), zp(N_HEADS, B_DK, HEAD_DIM),
            zp(N_HEADS, C_EXPAND, HEAD_DIM), zp(D_CONV_W - 1, GROUP_W), params,
            PROMPT_CHUNK, PROMPT_CHUNK, PROMPT_SEQS, f"mixers_p{l}")
        yp = _outproj(mix_p.reshape(bp * tp, D_MIX), w_out_b, ln_post, l, yp, f"outproj_p{l}")

        (proj_s,) = _rms_matmul(ys_pad, ln_pre, w_in_pt, l, (D_PROJ_P,), f"inproj_s{l}", True, PROJ_DTYPE)
        mix_s, s_wkv, s_shift, s_gla, s_hgrn, s_conv = _mixers(
            proj_s.reshape(bs, SAMPLE_CHUNK, D_PROJ_P), cache_kt, cache_vt, l,
            state_a_wkv[l], state_a_shift[l].reshape(bs, 1, A_SHIFT),
            state_b_gla[l], state_c_hgrn[l], state_d_conv[l],
            params, SAMPLE_CHUNK, ts, SAMPLE_SEQS, f"mixers_s{l}")
        ys_pad = _outproj(mix_s.reshape(bs * SAMPLE_CHUNK, D_MIX), w_out_b, ln_post, l, ys_pad,
                          f"outproj_s{l}")

        for lst, val in zip(outs, (
                p_wkv, p_shift.reshape(bp, A_SHIFT), p_gla, p_hgrn, p_conv, mkt, mvt,
                s_wkv, s_shift.reshape(bs, A_SHIFT), s_gla, s_hgrn, s_conv)):
            lst.append(val)

    y_prompt = yp.reshape(bp, tp, D_MODEL)
    y_sample = ys_pad.reshape(bs, SAMPLE_CHUNK, D_MODEL)[:, :ts]
    stacked = [jnp.stack(o) for o in outs]
    for i in (5, 6):
        stacked[i] = jnp.transpose(stacked[i], (0, 1, 4, 2, 3))
    return (y_prompt, y_sample) + tuple(stacked)
```

```python
import functools

import numpy as np
import jax
import jax.numpy as jnp
from jax import lax
from jax.experimental import pallas as pl
from jax.experimental.pallas import tpu as pltpu

F32 = jnp.float32
BF16 = jnp.bfloat16

D_MODEL = 1024
DEPTH = 4
N_MEM = 256
GROUP_W = 256
N_HEADS = 4
HEAD_DIM = 64
A_LORA = 64
A_SHIFT = 3 * GROUP_W + 2 * A_LORA
A_GN_EPS = 64e-5
B_DK = 32
B_QK = N_HEADS * B_DK
B_LORA_G = 16
B_GATE_NORM = 16.0
C_EXPAND = 64
C_F = N_HEADS * C_EXPAND
D_CONV_W = 3
D_MIX = 5 * GROUP_W
EPS = 1e-6
LANES = 128
SUBLANES = 8

_REF_WIDTHS = (A_SHIFT, GROUP_W, B_QK, B_QK, GROUP_W, B_LORA_G, GROUP_W, C_F, C_F, GROUP_W, GROUP_W,
               GROUP_W, GROUP_W, GROUP_W, GROUP_W, GROUP_W, GROUP_W)
_REF_OFF = np.concatenate([[0], np.cumsum(_REF_WIDTHS)]).tolist()
_NAMES = ("a_x", "a_z", "b_q", "b_k", "b_v", "b_gl", "b_z", "c_q", "c_f", "c_i", "c_z",
          "d_b", "d_c", "d_x", "d_z", "e_q", "e_z")
_PACK_ORDER = ("a_x", "a_z", "b_q", "b_k", "b_v", "b_z", "c_q", "c_f", "c_i", "c_z",
               "d_b", "d_c", "d_x", "d_z", "e_q", "e_z", "b_gl")
_OFF = {}
_o = 0
for _n in _PACK_ORDER:
    _w = _REF_WIDTHS[_NAMES.index(_n)]
    _OFF[_n] = (_o, _o + _w)
    _o += -(-_w // LANES) * LANES
D_PROJ_P = _o

PROMPT_CHUNK = 64
SAMPLE_CHUNK = 8
PROMPT_SEQS = 4
SAMPLE_SEQS = 8
ROW_TILE_IN = 512
VMEM_LIMIT = 48 * 1024 * 1024
PROJ_DTYPE = BF16


def _bf(x):
    return x.astype(BF16)


def _mm(a, b):
    return lax.dot_general(_bf(a), _bf(b), (((1,), (0,)), ((), ())), preferred_element_type=F32)


def _mm_nt(a, b):
    return lax.dot_general(_bf(a), _bf(b), (((1,), (1,)), ((), ())), preferred_element_type=F32)


def _mm_tn(a, b):
    return lax.dot_general(_bf(a), _bf(b), (((0,), (0,)), ((), ())), preferred_element_type=F32)


def _rows(shape):
    return lax.broadcasted_iota(jnp.int32, shape, 0)


def _cols(shape):
    return lax.broadcasted_iota(jnp.int32, shape, 1)


def _shift_rows(x, s):
    if s == 0:
        return x
    return jnp.where(_rows(x.shape) >= s, pltpu.roll(x, s, axis=0), 0.0)


def _cumsum_rows(x):
    s = 1
    while s < x.shape[0]:
        x = x + _shift_rows(x, s)
        s *= 2
    return x


def _softplus(x):
    return jnp.maximum(x, 0.0) + jnp.log(1.0 + jnp.exp(-jnp.abs(x)))


def _log_sigmoid(x):
    return -_softplus(-x)


def _silu(x):
    return x * jax.nn.sigmoid(x)


def _head_sum(x, ones_bd):
    hi = _bf(x)
    lo = _bf(x - hi.astype(F32))
    return (jnp.dot(hi, ones_bd, preferred_element_type=F32)
            + jnp.dot(lo, ones_bd, preferred_element_type=F32))


def _log2(n):
    return int(np.log2(n))


def _rms_matmul_kernel(w_transposed, x_ref, g_ref, w_ref, *o_refs):
    x = x_ref[...]
    u = _bf(x * lax.rsqrt(jnp.mean(x * x, axis=-1, keepdims=True) + EPS) * g_ref[...])
    off = 0
    for o_ref in o_refs:
        n = o_ref.shape[-1]
        if w_transposed:
            y = lax.dot_general(u, w_ref[off:off + n, :], (((1,), (1,)), ((), ())),
                                preferred_element_type=F32)
        else:
            y = jnp.dot(u, w_ref[:, off:off + n], preferred_element_type=F32)
        o_ref[...] = y.astype(o_ref.dtype)
        off += n


def _rms_matmul(x, g_all, w_all, layer, out_widths, name, w_transposed, out_dtype):
    n, d = x.shape
    tm = min(ROW_TILE_IN, n)
    return pl.pallas_call(
        functools.partial(_rms_matmul_kernel, w_transposed),
        grid=(n // tm,),
        in_specs=[pl.BlockSpec((tm, d), lambda i: (i, 0)),
                  pl.BlockSpec((None, 1, d), lambda i: (layer, 0, 0)),
                  pl.BlockSpec((None,) + w_all.shape[1:], lambda i: (layer, 0, 0),
                               pipeline_mode=pl.Buffered(1))],
        out_specs=[pl.BlockSpec((tm, ow), lambda i: (i, 0)) for ow in out_widths],
        out_shape=[jax.ShapeDtypeStruct((n, ow), out_dtype) for ow in out_widths],
        compiler_params=pltpu.CompilerParams(dimension_semantics=("arbitrary",),
                                             vmem_limit_bytes=VMEM_LIMIT),
        name=name,
    )(x, g_all.reshape(g_all.shape[0], 1, d), w_all)


def _interleave(tasks):
    results = [None] * len(tasks)
    live = list(range(len(tasks)))
    while live:
        still = []
        for i in live:
            try:
                next(tasks[i])
                still.append(i)
            except StopIteration as done:
                results[i] = done.value
        live = still
    return results


def _rwkv_head(r, v, k, at, rt, bt, kt, bh, kh, e_end_c, s0):
    c = r.shape[0]
    ri, ci = _rows((c, c)), _cols((c, c))
    strict, incl = ci < ri, ci <= ri
    ar = jnp.concatenate([at, rt], axis=0)
    gb = _mm_nt(ar, bt)
    gk = _mm_nt(ar, kt)
    w0 = _mm_nt(ar, s0)
    yield
    u = w0[:c] + _mm(jnp.where(strict, gk[:c], 0.0), v)
    p, n = jnp.where(strict, gb[:c], 0.0), 1
    yield
    while n < c:
        u = u + _mm(p, u)
        n *= 2
        if n < c:
            p = _mm(p, p)
        yield
    o = w0[c:] + _mm(jnp.where(incl, gb[c:], 0.0), u) + _mm(jnp.where(incl, gk[c:], 0.0), v)
    s_new = s0 * e_end_c + _mm_tn(u, bh) + _mm_tn(v, kh)
    return o, s_new


def _gla_chunk(q, k, v, lg, st, dk, ones_bd):
    c, wk = q.shape
    b = _cumsum_rows(lg)
    b_c = b[c - 1:c, :]
    o = _mm_nt(q * jnp.exp(b), st)
    rows = _rows((c, wk))
    sj_head = lax.shift_right_logical(_rows((GROUP_W, wk)), _log2(HEAD_DIM))
    sk_head = lax.shift_right_logical(_cols((GROUP_W, wk)), _log2(dk))
    st_new = st * jnp.exp(b_c) + jnp.where(sj_head == sk_head, _mm_tn(v, k * jnp.exp(b_c - b)), 0.0)

    in_blk = jnp.bitwise_and(rows, SUBLANES - 1)
    terms = [q * k]
    for d in range(1, SUBLANES):
        kd, bd = pltpu.roll(k, d, axis=0), pltpu.roll(b, d, axis=0)
        terms.append(jnp.where(in_blk >= d, q * kd * jnp.exp(jnp.minimum(b - bd, 0.0)), 0.0))
    s_all = _mm(jnp.concatenate(terms, axis=0), ones_bd)
    yield

    if c == SUBLANES:
        for d in range(SUBLANES):
            o = o + s_all[d * c:(d + 1) * c] * (v if d == 0 else pltpu.roll(v, d, axis=0))
        return o, st_new

    assert c == HEAD_DIM, "the attention matrix shares the 64-lane head segments of ones_bd"
    ai, aj = _rows((c, N_HEADS * c)), jnp.bitwise_and(_cols((c, N_HEADS * c)), c - 1)
    lag = ai - aj
    att = jnp.where(lag == 0, s_all[:c], 0.0)
    for d in range(1, SUBLANES):
        att = jnp.where(lag == d, s_all[d * c:(d + 1) * c], att)
    half = c // 2
    hj_head = lax.shift_right_logical(_rows((N_HEADS * c, wk)), _log2(c))
    k_head = lax.shift_right_logical(_cols((N_HEADS * c, wk)), _log2(dk))
    kmask = hj_head == k_head
    while half >= SUBLANES:
        blk = 2 * half
        nb = c // blk
        mid = jnp.broadcast_to(b.reshape(nb, blk, wk)[:, half - 1:half, :], (nb, blk, wk)).reshape(c, wk)
        second = jnp.bitwise_and(rows, blk - 1) >= half
        ql = jnp.where(second, q * jnp.exp(jnp.where(second, b - mid, 0.0)), 0.0)
        kl = jnp.where(second, 0.0, k * jnp.exp(jnp.where(second, 0.0, mid - b)))
        kbd = jnp.where(kmask, jnp.concatenate([kl] * N_HEADS, axis=0), 0.0)
        lvl = _mm_nt(ql, kbd)
        same = lax.shift_right_logical(ai, _log2(blk)) == lax.shift_right_logical(aj, _log2(blk))
        lvl = jnp.where(same, lvl, 0.0)
        att = att + lvl
        half //= 2
        yield
    vj_head = lax.shift_right_logical(_rows((N_HEADS * c, GROUP_W)), _log2(c))
    vv_head = lax.shift_right_logical(_cols((N_HEADS * c, GROUP_W)), _log2(HEAD_DIM))
    vbd = jnp.where(vj_head == vv_head, jnp.concatenate([v] * N_HEADS, axis=0), 0.0)
    o = o + _mm(att, vbd)
    yield
    return o, st_new


def _mem_attention(qe, kbd, vbd):
    sc = _mm(qe, kbd)
    yield
    prs = []
    for h in range(N_HEADS):
        s_h = sc[:, h * N_MEM:(h + 1) * N_MEM]
        e_h = jnp.exp(s_h - jnp.max(s_h, axis=-1, keepdims=True))
        prs.append(e_h / jnp.sum(e_h, axis=-1, keepdims=True))
    oe = _mm_nt(jnp.concatenate(prs, axis=-1), vbd)
    yield
    return oe


def _mixers_kernel(t_valid, nc,
                   proj_ref, xres_ref, wout_ref, lnpost_ref, mk_ref, mv_ref,
                   wkv0_ref, shift0_ref, gla0_ref, hgrn0_ref, conv0_ref,
                   mu_ref, w0_ref, w2_ref, a0_ref, a2_ref, akk_ref, aka_ref, ark_ref, agg_ref, agb_ref,
                   g2_ref, g2b_ref, bgg_ref, lb_ref, cgg_ref, cw_ref, gsum_ref, hsum_ref,
                   y_ref, wkv_ref, shift_ref, gla_ref, hgrn_ref, conv_ref,
                   kbd_ref, vbd_ref, glat_ref, hgrnt_ref):
    n_seq, c = proj_ref.shape[0], proj_ref.shape[1]
    step = pl.program_id(1)

    @pl.when(step == 0)
    def _():
        wkv_ref[...] = wkv0_ref[...]
        shift_ref[...] = shift0_ref[...]
        conv_ref[...] = conv0_ref[...]
        glat_ref[...] = jnp.zeros(glat_ref.shape, F32)
        hgrnt_ref[...] = jnp.zeros(hgrnt_ref.shape, F32)
        kbd_ref[...] = jnp.zeros(kbd_ref.shape, BF16)
        vbd_ref[...] = jnp.zeros(vbd_ref.shape, BF16)
        for g in range(n_seq):
            for h in range(N_HEADS):
                rs = slice(h * HEAD_DIM, (h + 1) * HEAD_DIM)
                ms = slice(h * N_MEM, (h + 1) * N_MEM)
                kbd_ref[g, rs, ms] = _bf(mk_ref[g, h])
                vbd_ref[g, rs, ms] = _bf(mv_ref[g, h])
                glat_ref[g, rs, h * B_DK:(h + 1) * B_DK] = gla0_ref[g, h].T
                hgrnt_ref[g, rs, h * C_EXPAND:(h + 1) * C_EXPAND] = hgrn0_ref[g, h].T

    row1 = _rows((c, 1))
    pad = t_valid < c
    valid = row1 < t_valid
    last = (t_valid if pad else c) - 1

    def prepare(g, st):
        def seg(name):
            lo, hi = _OFF[name]
            return proj_ref[g, :, lo:hi].astype(F32)

        ax = seg("a_x")
        prev = _shift_rows(ax, 1) + jnp.where(row1 == 0, st["shift"], 0.0)
        amix = ax + (prev - ax) * mu_ref[...]
        r = amix[:, 0:GROUP_W]
        k = amix[:, GROUP_W:2 * GROUP_W]
        v = amix[:, 2 * GROUP_W:3 * GROUP_W]
        wl = amix[:, 3 * GROUP_W:3 * GROUP_W + A_LORA]
        al = amix[:, 3 * GROUP_W + A_LORA:A_SHIFT]
        w_raw = -_softplus(-(w0_ref[...] + _mm(jnp.tanh(wl), w2_ref[...]))) - 0.5
        lw = -jnp.exp(w_raw)
        a = jax.nn.sigmoid(a0_ref[...] + _mm(al, a2_ref[...]))
        kk = k * akk_ref[...]
        kk = kk * lax.rsqrt(jnp.maximum(_head_sum(kk * kk, hsum_ref[...]), 1e-24))
        k = k * (1.0 + (a - 1.0) * aka_ref[...])
        bb = kk * a
        if pad:
            lw = jnp.where(valid, lw, 0.0)
            bb = jnp.where(valid, bb, 0.0)
            k = jnp.where(valid, k, 0.0)
        cum = _cumsum_rows(lw)
        cum_c = cum[c - 1:c, :]
        e_cum, e_inv, e_end = jnp.exp(cum), jnp.exp(-cum), jnp.exp(cum_c - cum)
        at = -kk * jnp.exp(cum - lw)
        rt = r * e_cum
        bt, kt = bb * e_inv, k * e_inv
        bh, kh = bb * e_end, k * e_end
        e_end_c = jnp.exp(cum_c)
        tasks = []
        for h in range(N_HEADS):
            sl = slice(h * HEAD_DIM, (h + 1) * HEAD_DIM)
            tasks.append(_rwkv_head(r[:, sl], v[:, sl], k[:, sl], at[:, sl], rt[:, sl], bt[:, sl],
                                    kt[:, sl], bh[:, sl], kh[:, sl], e_end_c[:, sl], st["wkv"][h]))

        bq = seg("b_q") * (B_DK ** -0.5)
        bk = seg("b_k")
        lo, _ = _OFF["b_gl"]
        lg = (_log_sigmoid(_mm(proj_ref[g, :, lo:lo + LANES].astype(F32), g2_ref[...]) + g2b_ref[...])
              * (1.0 / B_GATE_NORM))
        if pad:
            lg = jnp.where(valid, lg, 0.0)
            bk = jnp.where(valid, bk, 0.0)
        tasks.append(_gla_chunk(bq, bk, seg("b_v"), lg, st["gla"], B_DK, gsum_ref[...]))

        lb = lb_ref[...]
        cq = _silu(seg("c_q")) * (C_EXPAND ** -0.5)
        cf = seg("c_f")
        la, lbb = jnp.log(lb), jnp.log1p(-lb) + _log_sigmoid(cf)
        log_f = jnp.maximum(la, lbb) + jnp.log(1.0 + jnp.exp(-jnp.abs(la - lbb)))
        kc = (1.0 - lb) * jax.nn.sigmoid(-cf)
        if pad:
            log_f = jnp.where(valid, log_f, 0.0)
            kc = jnp.where(valid, kc, 0.0)
        tasks.append(_gla_chunk(cq, kc, seg("c_i"), log_f, st["hgrn"], C_EXPAND, hsum_ref[...]))

        tasks.append(_mem_attention(seg("e_q") * (HEAD_DIM ** -0.5), kbd_ref[g], vbd_ref[g]))

        def finish(res):
            new = {"wkv": [s_h for _, s_h in res[:N_HEADS]], "shift": ax[last:last + 1, :]}
            o_a = jnp.concatenate([o_h for o_h, _ in res[:N_HEADS]], axis=-1)
            mean = _head_sum(o_a, hsum_ref[...]) * (1.0 / HEAD_DIM)
            var = _head_sum(jnp.square(o_a - mean), hsum_ref[...]) * (1.0 / HEAD_DIM)
            o_a = (o_a - mean) * lax.rsqrt(var + A_GN_EPS) * agg_ref[...] + agb_ref[...]
            bonus = _head_sum(r * k * ark_ref[...], hsum_ref[...]) * v
            out_a = (o_a + bonus) * _silu(seg("a_z"))

            o_b, new["gla"] = res[N_HEADS]
            o_b = o_b * lax.rsqrt(_head_sum(o_b * o_b, hsum_ref[...]) * (1.0 / HEAD_DIM) + EPS)
            out_b = o_b * bgg_ref[...] * _silu(seg("b_z"))

            o_c, new["hgrn"] = res[N_HEADS + 1]
            o_c = o_c * lax.rsqrt(jnp.mean(o_c * o_c, axis=-1, keepdims=True) + EPS) * cgg_ref[...]
            out_c = o_c * _silu(seg("c_z"))

            cu = seg("d_c") * seg("d_x")
            p1 = _shift_rows(cu, 1) + jnp.where(row1 == 0, st["conv"][1:2, :], 0.0)
            p2 = (_shift_rows(cu, 2) + jnp.where(row1 == 0, st["conv"][0:1, :], 0.0)
                  + jnp.where(row1 == 1, st["conv"][1:2, :], 0.0))
            conv = p2 * cw_ref[0:1, :] + p1 * cw_ref[1:2, :] + cu * cw_ref[2:3, :]
            out_d = seg("d_b") * conv * _silu(seg("d_z"))
            new["conv"] = cu[last - 1:last + 1, :]

            out_e = res[N_HEADS + 2] * _silu(seg("e_z"))
            new["mix"] = jnp.concatenate([out_a, out_b, out_c, out_d, out_e], axis=-1)
            return new

        return tasks, finish

    olds = [{"wkv": [wkv_ref[g, h] for h in range(N_HEADS)], "shift": shift_ref[g], "gla": glat_ref[g],
             "hgrn": hgrnt_ref[g], "conv": conv_ref[g]} for g in range(n_seq)]
    prepared = [prepare(g, olds[g]) for g in range(n_seq)]
    n_tasks = N_HEADS + 3
    results = _interleave([t for tasks, _ in prepared for t in tasks])
    mixes = []
    for g, (_, finish) in enumerate(prepared):
        new = finish(results[g * n_tasks:(g + 1) * n_tasks])
        for h in range(N_HEADS):
            wkv_ref[g, h] = new["wkv"][h]
        shift_ref[g] = new["shift"]
        glat_ref[g] = new["gla"]
        hgrnt_ref[g] = new["hgrn"]
        conv_ref[g] = new["conv"]
        mixes.append(new["mix"])

    z = jnp.dot(_bf(jnp.concatenate(mixes, axis=0)), wout_ref[...], preferred_element_type=F32)
    z = z * lax.rsqrt(jnp.mean(z * z, axis=-1, keepdims=True) + EPS) * lnpost_ref[...]
    y_ref[...] = xres_ref[...] + z.reshape(n_seq, c, D_MODEL)

    @pl.when(step == nc - 1)
    def _():
        for g in range(n_seq):
            for h in range(N_HEADS):
                rs = slice(h * HEAD_DIM, (h + 1) * HEAD_DIM)
                gla_ref[g, h] = glat_ref[g, rs, h * B_DK:(h + 1) * B_DK].T
                hgrn_ref[g, h] = hgrnt_ref[g, rs, h * C_EXPAND:(h + 1) * C_EXPAND].T


def _mixers(proj, xres, w_out_all, ln_post_all, layer, mem_kt, mem_vt, mem_layer, wkv0, shift0, gla0, hgrn0,
            conv0, params, chunk, t_valid, n_seq, name):
    s, t, _ = proj.shape
    nc = t // chunk
    seq3 = lambda i, j: (i, 0, 0)
    seq4 = lambda i, j: (i, 0, 0, 0)
    mem5 = lambda i, j: (mem_layer, i, 0, 0, 0)
    par = lambda i, j: (0, 0)
    state_blocks = [((n_seq, N_HEADS, HEAD_DIM, HEAD_DIM), seq4),
                    ((n_seq, 1, A_SHIFT), seq3),
                    ((n_seq, N_HEADS, B_DK, HEAD_DIM), seq4),
                    ((n_seq, N_HEADS, C_EXPAND, HEAD_DIM), seq4),
                    ((n_seq, D_CONV_W - 1, GROUP_W), seq3)]
    in_specs = [pl.BlockSpec((n_seq, chunk, D_PROJ_P), lambda i, j: (i, j, 0)),
                pl.BlockSpec((n_seq, chunk, D_MODEL), lambda i, j: (i, j, 0)),
                pl.BlockSpec((None, D_MIX, D_MODEL), lambda i, j: (layer, 0, 0), pipeline_mode=pl.Buffered(1)),
                pl.BlockSpec((None, 1, D_MODEL), lambda i, j: (layer, 0, 0)),
                pl.BlockSpec((None, n_seq, N_HEADS, HEAD_DIM, N_MEM), mem5),
                pl.BlockSpec((None, n_seq, N_HEADS, HEAD_DIM, N_MEM), mem5)]
    in_specs += [pl.BlockSpec(b, m) for b, m in state_blocks]
    in_specs += [pl.BlockSpec(p.shape, par) for p in params]
    out_specs = [pl.BlockSpec((n_seq, chunk, D_MODEL), lambda i, j: (i, j, 0))]
    out_specs += [pl.BlockSpec(b, m) for b, m in state_blocks]
    out_shape = [jax.ShapeDtypeStruct((s, t, D_MODEL), F32)]
    out_shape += [jax.ShapeDtypeStruct((s,) + b[1:], F32) for b, _ in state_blocks]
    return pl.pallas_call(
        functools.partial(_mixers_kernel, t_valid, nc),
        grid=(s // n_seq, nc),
        in_specs=in_specs,
        out_specs=out_specs,
        out_shape=out_shape,
        scratch_shapes=[pltpu.VMEM((n_seq, GROUP_W, N_HEADS * N_MEM), BF16),
                        pltpu.VMEM((n_seq, GROUP_W, N_HEADS * N_MEM), BF16),
                        pltpu.VMEM((n_seq, GROUP_W, B_QK), F32),
                        pltpu.VMEM((n_seq, GROUP_W, C_F), F32)],
        compiler_params=pltpu.CompilerParams(dimension_semantics=("arbitrary", "arbitrary"),
                                             vmem_limit_bytes=VMEM_LIMIT),
        name=name,
    )(proj, xres, w_out_all, ln_post_all.reshape(ln_post_all.shape[0], 1, D_MODEL), mem_kt, mem_vt,
      wkv0, shift0, gla0, hgrn0, conv0, *params)


def kernel(x_prompt, x_sample, mem_prompt, state_a_wkv, state_a_shift, state_b_gla, state_c_hgrn,
           state_d_conv, cache_mem_k, cache_mem_v, ln_pre, ln_post, w_in, w_out, a_mu, a_w0, a_w2,
           a_a0, a_a2, a_kk, a_ka, a_rk, a_gn_g, a_gn_b, b_g2, b_g2_bias, b_gn_g, c_lb_logits,
           c_gn_g, d_conv_w, e_mem_g, e_wk, e_wv):
    bp, tp, _ = x_prompt.shape
    bs, ts, _ = x_sample.shape
    assert tp % PROMPT_CHUNK == 0 and ts <= SAMPLE_CHUNK and ts >= D_CONV_W - 1
    assert bp % PROMPT_SEQS == 0 and bs % SAMPLE_SEQS == 0

    w_in_t = jnp.transpose(w_in, (0, 2, 1))
    segs = [w_in_t[:, _REF_OFF[_NAMES.index(n)]:_REF_OFF[_NAMES.index(n) + 1], :] for n in _PACK_ORDER]
    segs.append(jnp.zeros((DEPTH, D_PROJ_P - w_in.shape[-1], D_MODEL), w_in.dtype))
    w_in_pt = _bf(jnp.concatenate(segs, axis=1))
    w_out_b = _bf(w_out)
    cache_kt = jnp.transpose(cache_mem_k, (0, 1, 3, 4, 2))
    cache_vt = jnp.transpose(cache_mem_v, (0, 1, 3, 4, 2))
    w_kv = _bf(jnp.concatenate([e_wk, e_wv], axis=-1))
    g2_p = jnp.concatenate([b_g2, jnp.zeros((DEPTH, LANES - B_LORA_G, B_QK), b_g2.dtype)], axis=1)
    lb_cum = jnp.cumsum(jax.nn.softmax(c_lb_logits.astype(F32), axis=0), axis=0)
    lb_all = lb_cum - lb_cum[:1]

    def row(p):
        return p.reshape(1, -1)

    def head_blocks(rows_per_head):
        rh = np.arange(N_HEADS * rows_per_head)[:, None] // rows_per_head
        return jnp.asarray(rh == np.arange(GROUP_W)[None, :] // HEAD_DIM, dtype=BF16)

    gsum, hsum = head_blocks(B_DK), head_blocks(HEAD_DIM)

    zp = lambda *sh: jnp.zeros((bp,) + sh, F32)
    ys_pad = jnp.pad(x_sample, ((0, 0), (0, SAMPLE_CHUNK - ts), (0, 0))).reshape(bs * SAMPLE_CHUNK, D_MODEL)
    yp = x_prompt.reshape(bp * tp, D_MODEL)
    mem2 = mem_prompt.reshape(bp * N_MEM, D_MODEL)

    outs = [[] for _ in range(12)]
    for l in range(DEPTH):
        params = [row(a_mu[l]), row(a_w0[l]), a_w2[l], row(a_a0[l]), a_a2[l], row(a_kk[l]), row(a_ka[l]),
                  row(a_rk[l]), row(a_gn_g[l]), row(a_gn_b[l]), g2_p[l], row(b_g2_bias[l]),
                  row(b_gn_g[l]), row(lb_all[l]), row(c_gn_g[l]), d_conv_w[l], gsum, hsum]
        mk, mv = _rms_matmul(mem2, e_mem_g, w_kv, l, (GROUP_W, GROUP_W), f"memkv{l}", False, F32)
        mkt = jnp.transpose(mk.reshape(bp, N_MEM, N_HEADS, HEAD_DIM), (0, 2, 3, 1))
        mvt = jnp.transpose(mv.reshape(bp, N_MEM, N_HEADS, HEAD_DIM), (0, 2, 3, 1))

        (proj_p,) = _rms_matmul(yp, ln_pre, w_in_pt, l, (D_PROJ_P,), f"inproj_p{l}", True, PROJ_DTYPE)
        yp3, p_wkv, p_shift, p_gla, p_hgrn, p_conv = _mixers(
            proj_p.reshape(bp, tp, D_PROJ_P), yp.reshape(bp, tp, D_MODEL), w_out_b, ln_post, l,
            mkt[None], mvt[None], 0,
            zp(N_HEADS, HEAD_DIM, HEAD_DIM), zp(1, A_SHIFT), zp(N_HEADS, B_DK, HEAD_DIM),
            zp(N_HEADS, C_EXPAND, HEAD_DIM), zp(D_CONV_W - 1, GROUP_W), params,
            PROMPT_CHUNK, PROMPT_CHUNK, PROMPT_SEQS, f"mixers_p{l}")
        yp = yp3.reshape(bp * tp, D_MODEL)

        (proj_s,) = _rms_matmul(ys_pad, ln_pre, w_in_pt, l, (D_PROJ_P,), f"inproj_s{l}", True, PROJ_DTYPE)
        ys3, s_wkv, s_shift, s_gla, s_hgrn, s_conv = _mixers(
            proj_s.reshape(bs, SAMPLE_CHUNK, D_PROJ_P), ys_pad.reshape(bs, SAMPLE_CHUNK, D_MODEL),
            w_out_b, ln_post, l, cache_kt, cache_vt, l,
            state_a_wkv[l], state_a_shift[l].reshape(bs, 1, A_SHIFT),
            state_b_gla[l], state_c_hgrn[l], state_d_conv[l],
            params, SAMPLE_CHUNK, ts, SAMPLE_SEQS, f"mixers_s{l}")
        ys_pad = ys3.reshape(bs * SAMPLE_CHUNK, D_MODEL)

        for lst, val in zip(outs, (
                p_wkv, p_shift.reshape(bp, A_SHIFT), p_gla, p_hgrn, p_conv, mkt, mvt,
                s_wkv, s_shift.reshape(bs, A_SHIFT), s_gla, s_hgrn, s_conv)):
            lst.append(val)

    y_prompt = yp.reshape(bp, tp, D_MODEL)
    y_sample = ys_pad.reshape(bs, SAMPLE_CHUNK, D_MODEL)[:, :ts]
    stacked = [jnp.stack(o) for o in outs]
    for i in (5, 6):
        stacked[i] = jnp.transpose(stacked[i], (0, 1, 4, 2, 3))
    return (y_prompt, y_sample) + tuple(stacked)
```

```python
import functools

import numpy as np
import jax
import jax.numpy as jnp
from jax import lax
from jax.experimental import pallas as pl
from jax.experimental.pallas import tpu as pltpu

F32 = jnp.float32
BF16 = jnp.bfloat16

D_MODEL = 1024
DEPTH = 4
N_MEM = 256
GROUP_W = 256
N_HEADS = 4
HEAD_DIM = 64
A_LORA = 64
A_SHIFT = 3 * GROUP_W + 2 * A_LORA
A_GN_EPS = 64e-5
B_DK = 32
B_QK = N_HEADS * B_DK
B_LORA_G = 16
B_GATE_NORM = 16.0
C_EXPAND = 64
C_F = N_HEADS * C_EXPAND
D_CONV_W = 3
D_MIX = 5 * GROUP_W
EPS = 1e-6
LANES = 128
SUBLANES = 8

_REF_WIDTHS = (A_SHIFT, GROUP_W, B_QK, B_QK, GROUP_W, B_LORA_G, GROUP_W, C_F, C_F, GROUP_W, GROUP_W,
               GROUP_W, GROUP_W, GROUP_W, GROUP_W, GROUP_W, GROUP_W)
_REF_OFF = np.concatenate([[0], np.cumsum(_REF_WIDTHS)]).tolist()
_NAMES = ("a_x", "a_z", "b_q", "b_k", "b_v", "b_gl", "b_z", "c_q", "c_f", "c_i", "c_z",
          "d_b", "d_c", "d_x", "d_z", "e_q", "e_z")
_PACK_ORDER = ("a_x", "a_z", "b_q", "b_k", "b_v", "b_z", "c_q", "c_f", "c_i", "c_z",
               "d_b", "d_c", "d_x", "d_z", "e_q", "e_z", "b_gl")
_OFF = {}
_o = 0
for _n in _PACK_ORDER:
    _w = _REF_WIDTHS[_NAMES.index(_n)]
    _OFF[_n] = (_o, _o + _w)
    _o += -(-_w // LANES) * LANES
D_PROJ_P = _o

PROMPT_CHUNK = 64
SAMPLE_CHUNK = 8
PROMPT_SEQS = 4
SAMPLE_SEQS = 8
ROW_TILE_IN = 1024
VMEM_LIMIT = 48 * 1024 * 1024
PROJ_DTYPE = BF16


def _bf(x):
    return x.astype(BF16)


def _mm(a, b):
    return lax.dot_general(_bf(a), _bf(b), (((1,), (0,)), ((), ())), preferred_element_type=F32)


def _mm_nt(a, b):
    return lax.dot_general(_bf(a), _bf(b), (((1,), (1,)), ((), ())), preferred_element_type=F32)


def _mm_tn(a, b):
    return lax.dot_general(_bf(a), _bf(b), (((0,), (0,)), ((), ())), preferred_element_type=F32)


def _rows(shape):
    return lax.broadcasted_iota(jnp.int32, shape, 0)


def _cols(shape):
    return lax.broadcasted_iota(jnp.int32, shape, 1)


def _shift_rows(x, s):
    if s == 0:
        return x
    return jnp.where(_rows(x.shape) >= s, pltpu.roll(x, s, axis=0), 0.0)


def _cumsum_rows(x):
    s = 1
    while s < x.shape[0]:
        x = x + _shift_rows(x, s)
        s *= 2
    return x


def _softplus(x):
    return jnp.maximum(x, 0.0) + jnp.log(1.0 + jnp.exp(-jnp.abs(x)))


def _log_sigmoid(x):
    return -_softplus(-x)


def _silu(x):
    return x * jax.nn.sigmoid(x)


def _head_sum(x, ones_bd):
    hi = _bf(x)
    lo = _bf(x - hi.astype(F32))
    return (jnp.dot(hi, ones_bd, preferred_element_type=F32)
            + jnp.dot(lo, ones_bd, preferred_element_type=F32))


def _log2(n):
    return int(np.log2(n))


def _rms_matmul_kernel(w_transposed, x_ref, g_ref, w_ref, *o_refs):
    x = x_ref[...]
    u = _bf(x * lax.rsqrt(jnp.mean(x * x, axis=-1, keepdims=True) + EPS) * g_ref[...])
    off = 0
    for o_ref in o_refs:
        n = o_ref.shape[-1]
        if w_transposed:
            y = lax.dot_general(u, w_ref[off:off + n, :], (((1,), (1,)), ((), ())),
                                preferred_element_type=F32)
        else:
            y = jnp.dot(u, w_ref[:, off:off + n], preferred_element_type=F32)
        o_ref[...] = y.astype(o_ref.dtype)
        off += n


def _rms_matmul(x, g_all, w_all, layer, out_widths, name, w_transposed, out_dtype):
    n, d = x.shape
    tm = min(ROW_TILE_IN, n)
    return pl.pallas_call(
        functools.partial(_rms_matmul_kernel, w_transposed),
        grid=(n // tm,),
        in_specs=[pl.BlockSpec((tm, d), lambda i: (i, 0)),
                  pl.BlockSpec((None, 1, d), lambda i: (layer, 0, 0)),
                  pl.BlockSpec((None,) + w_all.shape[1:], lambda i: (layer, 0, 0),
                               pipeline_mode=pl.Buffered(1))],
        out_specs=[pl.BlockSpec((tm, ow), lambda i: (i, 0)) for ow in out_widths],
        out_shape=[jax.ShapeDtypeStruct((n, ow), out_dtype) for ow in out_widths],
        compiler_params=pltpu.CompilerParams(dimension_semantics=("arbitrary",),
                                             vmem_limit_bytes=VMEM_LIMIT),
        name=name,
    )(x, g_all.reshape(g_all.shape[0], 1, d), w_all)


def _interleave(tasks):
    results = [None] * len(tasks)
    live = list(range(len(tasks)))
    while live:
        still = []
        for i in live:
            try:
                next(tasks[i])
                still.append(i)
            except StopIteration as done:
                results[i] = done.value
        live = still
    return results


def _rwkv_head(r, v, k, at, rt, bt, kt, bh, kh, e_end_c, s0):
    c = r.shape[0]
    ri, ci = _rows((c, c)), _cols((c, c))
    strict, incl = ci < ri, ci <= ri
    ar = jnp.concatenate([at, rt], axis=0)
    gb = _mm_nt(ar, bt)
    gk = _mm_nt(ar, kt)
    w0 = _mm_nt(ar, s0)
    yield
    u = w0[:c] + _mm(jnp.where(strict, gk[:c], 0.0), v)
    p, n = jnp.where(strict, gb[:c], 0.0), 1
    yield
    while n < c:
        u = u + _mm(p, u)
        n *= 2
        if n < c:
            p = _mm(p, p)
        yield
    o = w0[c:] + _mm(jnp.where(incl, gb[c:], 0.0), u) + _mm(jnp.where(incl, gk[c:], 0.0), v)
    s_new = s0 * e_end_c + _mm_tn(u, bh) + _mm_tn(v, kh)
    return o, s_new


def _gla_chunk(q, k, v, lg, st, dk, ones_bd):
    c, wk = q.shape
    b = _cumsum_rows(lg)
    b_c = b[c - 1:c, :]
    o = _mm_nt(q * jnp.exp(b), st)
    rows = _rows((c, wk))
    sj_head = lax.shift_right_logical(_rows((GROUP_W, wk)), _log2(HEAD_DIM))
    sk_head = lax.shift_right_logical(_cols((GROUP_W, wk)), _log2(dk))
    st_new = st * jnp.exp(b_c) + jnp.where(sj_head == sk_head, _mm_tn(v, k * jnp.exp(b_c - b)), 0.0)

    in_blk = jnp.bitwise_and(rows, SUBLANES - 1)
    terms = [q * k]
    for d in range(1, SUBLANES):
        kd, bd = pltpu.roll(k, d, axis=0), pltpu.roll(b, d, axis=0)
        terms.append(jnp.where(in_blk >= d, q * kd * jnp.exp(jnp.minimum(b - bd, 0.0)), 0.0))
    s_all = _mm(jnp.concatenate(terms, axis=0), ones_bd)
    yield

    if c == SUBLANES:
        for d in range(SUBLANES):
            o = o + s_all[d * c:(d + 1) * c] * (v if d == 0 else pltpu.roll(v, d, axis=0))
        return o, st_new

    assert c == HEAD_DIM, "the attention matrix shares the 64-lane head segments of ones_bd"
    ai, aj = _rows((c, N_HEADS * c)), jnp.bitwise_and(_cols((c, N_HEADS * c)), c - 1)
    lag = ai - aj
    att = jnp.where(lag == 0, s_all[:c], 0.0)
    for d in range(1, SUBLANES):
        att = jnp.where(lag == d, s_all[d * c:(d + 1) * c], att)
    half = c // 2
    hj_head = lax.shift_right_logical(_rows((N_HEADS * c, wk)), _log2(c))
    k_head = lax.shift_right_logical(_cols((N_HEADS * c, wk)), _log2(dk))
    kmask = hj_head == k_head
    while half >= SUBLANES:
        blk = 2 * half
        nb = c // blk
        mid = jnp.broadcast_to(b.reshape(nb, blk, wk)[:, half - 1:half, :], (nb, blk, wk)).reshape(c, wk)
        second = jnp.bitwise_and(rows, blk - 1) >= half
        ql = jnp.where(second, q * jnp.exp(jnp.where(second, b - mid, 0.0)), 0.0)
        kl = jnp.where(second, 0.0, k * jnp.exp(jnp.where(second, 0.0, mid - b)))
        kbd = jnp.where(kmask, jnp.concatenate([kl] * N_HEADS, axis=0), 0.0)
        lvl = _mm_nt(ql, kbd)
        same = lax.shift_right_logical(ai, _log2(blk)) == lax.shift_right_logical(aj, _log2(blk))
        lvl = jnp.where(same, lvl, 0.0)
        att = att + lvl
        half //= 2
        yield
    vj_head = lax.shift_right_logical(_rows((N_HEADS * c, GROUP_W)), _log2(c))
    vv_head = lax.shift_right_logical(_cols((N_HEADS * c, GROUP_W)), _log2(HEAD_DIM))
    vbd = jnp.where(vj_head == vv_head, jnp.concatenate([v] * N_HEADS, axis=0), 0.0)
    o = o + _mm(att, vbd)
    yield
    return o, st_new


def _mem_attention(qe, kbd, vbd):
    sc = _mm(qe, kbd)
    yield
    prs = []
    for h in range(N_HEADS):
        s_h = sc[:, h * N_MEM:(h + 1) * N_MEM]
        e_h = jnp.exp(s_h - jnp.max(s_h, axis=-1, keepdims=True))
        prs.append(e_h / jnp.sum(e_h, axis=-1, keepdims=True))
    oe = _mm_nt(jnp.concatenate(prs, axis=-1), vbd)
    yield
    return oe


def _mixers_kernel(t_valid, nc,
                   proj_ref, xres_ref, wout_ref, lnpost_ref, mk_ref, mv_ref,
                   wkv0_ref, shift0_ref, gla0_ref, hgrn0_ref, conv0_ref,
                   mu_ref, w0_ref, w2_ref, a0_ref, a2_ref, akk_ref, aka_ref, ark_ref, agg_ref, agb_ref,
                   g2_ref, g2b_ref, bgg_ref, lb_ref, cgg_ref, cw_ref, gsum_ref, hsum_ref,
                   y_ref, wkv_ref, shift_ref, gla_ref, hgrn_ref, conv_ref,
                   kbd_ref, vbd_ref, glat_ref, hgrnt_ref):
    n_seq, c = proj_ref.shape[0], proj_ref.shape[1]
    step = pl.program_id(1)

    @pl.when(jnp.logical_and(pl.program_id(0) == 0, step == 0))
    def _():
        glat_ref[...] = jnp.zeros(glat_ref.shape, F32)
        hgrnt_ref[...] = jnp.zeros(hgrnt_ref.shape, F32)
        kbd_ref[...] = jnp.zeros(kbd_ref.shape, BF16)
        vbd_ref[...] = jnp.zeros(vbd_ref.shape, BF16)

    @pl.when(step == 0)
    def _():
        wkv_ref[...] = wkv0_ref[...]
        shift_ref[...] = shift0_ref[...]
        conv_ref[...] = conv0_ref[...]
        for g in range(n_seq):
            for h in range(N_HEADS):
                rs = slice(h * HEAD_DIM, (h + 1) * HEAD_DIM)
                ms = slice(h * N_MEM, (h + 1) * N_MEM)
                kbd_ref[g, rs, ms] = _bf(mk_ref[g, h])
                vbd_ref[g, rs, ms] = _bf(mv_ref[g, h])
                glat_ref[g, rs, h * B_DK:(h + 1) * B_DK] = gla0_ref[g, h].T
                hgrnt_ref[g, rs, h * C_EXPAND:(h + 1) * C_EXPAND] = hgrn0_ref[g, h].T

    row1 = _rows((c, 1))
    pad = t_valid < c
    valid = row1 < t_valid
    last = (t_valid if pad else c) - 1

    def prepare(g, st):
        def seg(name):
            lo, hi = _OFF[name]
            return proj_ref[g, :, lo:hi].astype(F32)

        ax = seg("a_x")
        prev = _shift_rows(ax, 1) + jnp.where(row1 == 0, st["shift"], 0.0)
        amix = ax + (prev - ax) * mu_ref[...]
        r = amix[:, 0:GROUP_W]
        k = amix[:, GROUP_W:2 * GROUP_W]
        v = amix[:, 2 * GROUP_W:3 * GROUP_W]
        wl = amix[:, 3 * GROUP_W:3 * GROUP_W + A_LORA]
        al = amix[:, 3 * GROUP_W + A_LORA:A_SHIFT]
        w_raw = -_softplus(-(w0_ref[...] + _mm(jnp.tanh(wl), w2_ref[...]))) - 0.5
        lw = -jnp.exp(w_raw)
        a = jax.nn.sigmoid(a0_ref[...] + _mm(al, a2_ref[...]))
        kk = k * akk_ref[...]
        kk = kk * lax.rsqrt(jnp.maximum(_head_sum(kk * kk, hsum_ref[...]), 1e-24))
        k = k * (1.0 + (a - 1.0) * aka_ref[...])
        bb = kk * a
        if pad:
            lw = jnp.where(valid, lw, 0.0)
            bb = jnp.where(valid, bb, 0.0)
            k = jnp.where(valid, k, 0.0)
        cum = _cumsum_rows(lw)
        cum_c = cum[c - 1:c, :]
        e_cum, e_inv, e_end = jnp.exp(cum), jnp.exp(-cum), jnp.exp(cum_c - cum)
        at = -kk * jnp.exp(cum - lw)
        rt = r * e_cum
        bt, kt = bb * e_inv, k * e_inv
        bh, kh = bb * e_end, k * e_end
        e_end_c = jnp.exp(cum_c)
        tasks = []
        for h in range(N_HEADS):
            sl = slice(h * HEAD_DIM, (h + 1) * HEAD_DIM)
            tasks.append(_rwkv_head(r[:, sl], v[:, sl], k[:, sl], at[:, sl], rt[:, sl], bt[:, sl],
                                    kt[:, sl], bh[:, sl], kh[:, sl], e_end_c[:, sl], st["wkv"][h]))

        bq = seg("b_q") * (B_DK ** -0.5)
        bk = seg("b_k")
        lo, _ = _OFF["b_gl"]
        lg = (_log_sigmoid(_mm(proj_ref[g, :, lo:lo + LANES].astype(F32), g2_ref[...]) + g2b_ref[...])
              * (1.0 / B_GATE_NORM))
        if pad:
            lg = jnp.where(valid, lg, 0.0)
            bk = jnp.where(valid, bk, 0.0)
        tasks.append(_gla_chunk(bq, bk, seg("b_v"), lg, st["gla"], B_DK, gsum_ref[...]))

        lb = lb_ref[...]
        cq = _silu(seg("c_q")) * (C_EXPAND ** -0.5)
        cf = seg("c_f")
        la, lbb = jnp.log(lb), jnp.log1p(-lb) + _log_sigmoid(cf)
        log_f = jnp.maximum(la, lbb) + jnp.log(1.0 + jnp.exp(-jnp.abs(la - lbb)))
        kc = (1.0 - lb) * jax.nn.sigmoid(-cf)
        if pad:
            log_f = jnp.where(valid, log_f, 0.0)
            kc = jnp.where(valid, kc, 0.0)
        tasks.append(_gla_chunk(cq, kc, seg("c_i"), log_f, st["hgrn"], C_EXPAND, hsum_ref[...]))

        tasks.append(_mem_attention(seg("e_q") * (HEAD_DIM ** -0.5), kbd_ref[g], vbd_ref[g]))

        def finish(res):
            new = {"wkv": [s_h for _, s_h in res[:N_HEADS]], "shift": ax[last:last + 1, :]}
            o_a = jnp.concatenate([o_h for o_h, _ in res[:N_HEADS]], axis=-1)
            mean = _head_sum(o_a, hsum_ref[...]) * (1.0 / HEAD_DIM)
            var = _head_sum(jnp.square(o_a - mean), hsum_ref[...]) * (1.0 / HEAD_DIM)
            o_a = (o_a - mean) * lax.rsqrt(var + A_GN_EPS) * agg_ref[...] + agb_ref[...]
            bonus = _head_sum(r * k * ark_ref[...], hsum_ref[...]) * v
            out_a = (o_a + bonus) * _silu(seg("a_z"))

            o_b, new["gla"] = res[N_HEADS]
            o_b = o_b * lax.rsqrt(_head_sum(o_b * o_b, hsum_ref[...]) * (1.0 / HEAD_DIM) + EPS)
            out_b = o_b * bgg_ref[...] * _silu(seg("b_z"))

            o_c, new["hgrn"] = res[N_HEADS + 1]
            o_c = o_c * lax.rsqrt(jnp.mean(o_c * o_c, axis=-1, keepdims=True) + EPS) * cgg_ref[...]
            out_c = o_c * _silu(seg("c_z"))

            cu = seg("d_c") * seg("d_x")
            p1 = _shift_rows(cu, 1) + jnp.where(row1 == 0, st["conv"][1:2, :], 0.0)
            p2 = (_shift_rows(cu, 2) + jnp.where(row1 == 0, st["conv"][0:1, :], 0.0)
                  + jnp.where(row1 == 1, st["conv"][1:2, :], 0.0))
            conv = p2 * cw_ref[0:1, :] + p1 * cw_ref[1:2, :] + cu * cw_ref[2:3, :]
            out_d = seg("d_b") * conv * _silu(seg("d_z"))
            new["conv"] = cu[last - 1:last + 1, :]

            out_e = res[N_HEADS + 2] * _silu(seg("e_z"))
            new["mix"] = jnp.concatenate([out_a, out_b, out_c, out_d, out_e], axis=-1)
            return new

        return tasks, finish

    olds = [{"wkv": [wkv_ref[g, h] for h in range(N_HEADS)], "shift": shift_ref[g], "gla": glat_ref[g],
             "hgrn": hgrnt_ref[g], "conv": conv_ref[g]} for g in range(n_seq)]
    prepared = [prepare(g, olds[g]) for g in range(n_seq)]
    n_tasks = N_HEADS + 3
    results = _interleave([t for tasks, _ in prepared for t in tasks])
    mixes = []
    for g, (_, finish) in enumerate(prepared):
        new = finish(results[g * n_tasks:(g + 1) * n_tasks])
        for h in range(N_HEADS):
            wkv_ref[g, h] = new["wkv"][h]
        shift_ref[g] = new["shift"]
        glat_ref[g] = new["gla"]
        hgrnt_ref[g] = new["hgrn"]
        conv_ref[g] = new["conv"]
        mixes.append(new["mix"])

    z = jnp.dot(_bf(jnp.concatenate(mixes, axis=0)), wout_ref[...], preferred_element_type=F32)
    z = z * lax.rsqrt(jnp.mean(z * z, axis=-1, keepdims=True) + EPS) * lnpost_ref[...]
    y_ref[...] = xres_ref[...] + z.reshape(n_seq, c, D_MODEL)

    @pl.when(step == nc - 1)
    def _():
        for g in range(n_seq):
            for h in range(N_HEADS):
                rs = slice(h * HEAD_DIM, (h + 1) * HEAD_DIM)
                gla_ref[g, h] = glat_ref[g, rs, h * B_DK:(h + 1) * B_DK].T
                hgrn_ref[g, h] = hgrnt_ref[g, rs, h * C_EXPAND:(h + 1) * C_EXPAND].T


def _mixers(proj, xres, w_out_all, ln_post_all, layer, mem_kt, mem_vt, mem_layer, wkv0, shift0, gla0, hgrn0,
            conv0, params, chunk, t_valid, n_seq, name):
    s, t, _ = proj.shape
    nc = t // chunk
    seq3 = lambda i, j: (i, 0, 0)
    seq4 = lambda i, j: (i, 0, 0, 0)
    mem5 = lambda i, j: (mem_layer, i, 0, 0, 0)
    par = lambda i, j: (0, 0)
    state_blocks = [((n_seq, N_HEADS, HEAD_DIM, HEAD_DIM), seq4),
                    ((n_seq, 1, A_SHIFT), seq3),
                    ((n_seq, N_HEADS, B_DK, HEAD_DIM), seq4),
                    ((n_seq, N_HEADS, C_EXPAND, HEAD_DIM), seq4),
                    ((n_seq, D_CONV_W - 1, GROUP_W), seq3)]
    in_specs = [pl.BlockSpec((n_seq, chunk, D_PROJ_P), lambda i, j: (i, j, 0)),
                pl.BlockSpec((n_seq, chunk, D_MODEL), lambda i, j: (i, j, 0)),
                pl.BlockSpec((None, D_MIX, D_MODEL), lambda i, j: (layer, 0, 0), pipeline_mode=pl.Buffered(1)),
                pl.BlockSpec((None, 1, D_MODEL), lambda i, j: (layer, 0, 0)),
                pl.BlockSpec((None, n_seq, N_HEADS, HEAD_DIM, N_MEM), mem5),
                pl.BlockSpec((None, n_seq, N_HEADS, HEAD_DIM, N_MEM), mem5)]
    in_specs += [pl.BlockSpec(b, m) for b, m in state_blocks]
    in_specs += [pl.BlockSpec(p.shape, par) for p in params]
    out_specs = [pl.BlockSpec((n_seq, chunk, D_MODEL), lambda i, j: (i, j, 0))]
    out_specs += [pl.BlockSpec(b, m) for b, m in state_blocks]
    out_shape = [jax.ShapeDtypeStruct((s, t, D_MODEL), F32)]
    out_shape += [jax.ShapeDtypeStruct((s,) + b[1:], F32) for b, _ in state_blocks]
    return pl.pallas_call(
        functools.partial(_mixers_kernel, t_valid, nc),
        grid=(s // n_seq, nc),
        in_specs=in_specs,
        out_specs=out_specs,
        out_shape=out_shape,
        scratch_shapes=[pltpu.VMEM((n_seq, GROUP_W, N_HEADS * N_MEM), BF16),
                        pltpu.VMEM((n_seq, GROUP_W, N_HEADS * N_MEM), BF16),
                        pltpu.VMEM((n_seq, GROUP_W, B_QK), F32),
                        pltpu.VMEM((n_seq, GROUP_W, C_F), F32)],
        compiler_params=pltpu.CompilerParams(dimension_semantics=("arbitrary", "arbitrary"),
                                             vmem_limit_bytes=VMEM_LIMIT),
        name=name,
    )(proj, xres, w_out_all, ln_post_all.reshape(ln_post_all.shape[0], 1, D_MODEL), mem_kt, mem_vt,
      wkv0, shift0, gla0, hgrn0, conv0, *params)


def kernel(x_prompt, x_sample, mem_prompt, state_a_wkv, state_a_shift, state_b_gla, state_c_hgrn,
           state_d_conv, cache_mem_k, cache_mem_v, ln_pre, ln_post, w_in, w_out, a_mu, a_w0, a_w2,
           a_a0, a_a2, a_kk, a_ka, a_rk, a_gn_g, a_gn_b, b_g2, b_g2_bias, b_gn_g, c_lb_logits,
           c_gn_g, d_conv_w, e_mem_g, e_wk, e_wv):
    bp, tp, _ = x_prompt.shape
    bs, ts, _ = x_sample.shape
    assert tp % PROMPT_CHUNK == 0 and ts <= SAMPLE_CHUNK and ts >= D_CONV_W - 1
    assert bp % PROMPT_SEQS == 0 and bs % SAMPLE_SEQS == 0

    w_in_t = jnp.transpose(w_in, (0, 2, 1))
    segs = [w_in_t[:, _REF_OFF[_NAMES.index(n)]:_REF_OFF[_NAMES.index(n) + 1], :] for n in _PACK_ORDER]
    segs.append(jnp.zeros((DEPTH, D_PROJ_P - w_in.shape[-1], D_MODEL), w_in.dtype))
    w_in_pt = _bf(jnp.concatenate(segs, axis=1))
    w_out_b = _bf(w_out)
    cache_kt = jnp.transpose(cache_mem_k, (0, 1, 3, 4, 2))
    cache_vt = jnp.transpose(cache_mem_v, (0, 1, 3, 4, 2))
    w_kv = _bf(jnp.concatenate([e_wk, e_wv], axis=-1))
    g2_p = jnp.concatenate([b_g2, jnp.zeros((DEPTH, LANES - B_LORA_G, B_QK), b_g2.dtype)], axis=1)
    lb_cum = jnp.cumsum(jax.nn.softmax(c_lb_logits.astype(F32), axis=0), axis=0)
    lb_all = lb_cum - lb_cum[:1]

    def row(p):
        return p.reshape(1, -1)

    def head_blocks(rows_per_head):
        rh = np.arange(N_HEADS * rows_per_head)[:, None] // rows_per_head
        return jnp.asarray(rh == np.arange(GROUP_W)[None, :] // HEAD_DIM, dtype=BF16)

    gsum, hsum = head_blocks(B_DK), head_blocks(HEAD_DIM)

    zp = lambda *sh: jnp.zeros((bp,) + sh, F32)
    ys_pad = jnp.pad(x_sample, ((0, 0), (0, SAMPLE_CHUNK - ts), (0, 0))).reshape(bs * SAMPLE_CHUNK, D_MODEL)
    yp = x_prompt.reshape(bp * tp, D_MODEL)
    mem2 = mem_prompt.reshape(bp * N_MEM, D_MODEL)

    outs = [[] for _ in range(12)]
    for l in range(DEPTH):
        params = [row(a_mu[l]), row(a_w0[l]), a_w2[l], row(a_a0[l]), a_a2[l], row(a_kk[l]), row(a_ka[l]),
                  row(a_rk[l]), row(a_gn_g[l]), row(a_gn_b[l]), g2_p[l], row(b_g2_bias[l]),
                  row(b_gn_g[l]), row(lb_all[l]), row(c_gn_g[l]), d_conv_w[l], gsum, hsum]
        mk, mv = _rms_matmul(mem2, e_mem_g, w_kv, l, (GROUP_W, GROUP_W), f"memkv{l}", False, F32)
        mkt = jnp.transpose(mk.reshape(bp, N_MEM, N_HEADS, HEAD_DIM), (0, 2, 3, 1))
        mvt = jnp.transpose(mv.reshape(bp, N_MEM, N_HEADS, HEAD_DIM), (0, 2, 3, 1))

        (proj_p,) = _rms_matmul(yp, ln_pre, w_in_pt, l, (D_PROJ_P,), f"inproj_p{l}", True, PROJ_DTYPE)
        yp3, p_wkv, p_shift, p_gla, p_hgrn, p_conv = _mixers(
            proj_p.reshape(bp, tp, D_PROJ_P), yp.reshape(bp, tp, D_MODEL), w_out_b, ln_post, l,
            mkt[None], mvt[None], 0,
            zp(N_HEADS, HEAD_DIM, HEAD_DIM), zp(1, A_SHIFT), zp(N_HEADS, B_DK, HEAD_DIM),
            zp(N_HEADS, C_EXPAND, HEAD_DIM), zp(D_CONV_W - 1, GROUP_W), params,
            PROMPT_CHUNK, PROMPT_CHUNK, PROMPT_SEQS, f"mixers_p{l}")
        yp = yp3.reshape(bp * tp, D_MODEL)

        (proj_s,) = _rms_matmul(ys_pad, ln_pre, w_in_pt, l, (D_PROJ_P,), f"inproj_s{l}", True, PROJ_DTYPE)
        ys3, s_wkv, s_shift, s_gla, s_hgrn, s_conv = _mixers(
            proj_s.reshape(bs, SAMPLE_CHUNK, D_PROJ_P), ys_pad.reshape(bs, SAMPLE_CHUNK, D_MODEL),
            w_out_b, ln_post, l, cache_kt, cache_vt, l,
            state_a_wkv[l], state_a_shift[l].reshape(bs, 1, A_SHIFT),
            state_b_gla[l], state_c_hgrn[l], state_d_conv[l],
            params, SAMPLE_CHUNK, ts, SAMPLE_SEQS, f"mixers_s{l}")
        ys_pad = ys3.reshape(bs * SAMPLE_CHUNK, D_MODEL)

        for lst, val in zip(outs, (
                p_wkv, p_shift.reshape(bp, A_SHIFT), p_gla, p_hgrn, p_conv, mkt, mvt,
                s_wkv, s_shift.reshape(bs, A_SHIFT), s_gla, s_hgrn, s_conv)):
            lst.append(val)

    y_prompt = yp.reshape(bp, tp, D_MODEL)
    y_sample = ys_pad.reshape(bs, SAMPLE_CHUNK, D_MODEL)[:, :ts]
    stacked = [jnp.stack(o) for o in outs]
    for i in (5, 6):
        stacked[i] = jnp.transpose(stacked[i], (0, 1, 4, 2, 3))
    return (y_prompt, y_sample) + tuple(stacked)
```

```python
import functools

import numpy as np
import jax
import jax.numpy as jnp
from jax import lax
from jax.experimental import pallas as pl
from jax.experimental.pallas import tpu as pltpu

F32 = jnp.float32
BF16 = jnp.bfloat16

D_MODEL = 1024
DEPTH = 4
N_MEM = 256
GROUP_W = 256
N_HEADS = 4
HEAD_DIM = 64
A_LORA = 64
A_SHIFT = 3 * GROUP_W + 2 * A_LORA
A_GN_EPS = 64e-5
B_DK = 32
B_QK = N_HEADS * B_DK
B_LORA_G = 16
B_GATE_NORM = 16.0
C_EXPAND = 64
C_F = N_HEADS * C_EXPAND
D_CONV_W = 3
D_MIX = 5 * GROUP_W
EPS = 1e-6
LANES = 128
SUBLANES = 8

_REF_WIDTHS = (A_SHIFT, GROUP_W, B_QK, B_QK, GROUP_W, B_LORA_G, GROUP_W, C_F, C_F, GROUP_W, GROUP_W,
               GROUP_W, GROUP_W, GROUP_W, GROUP_W, GROUP_W, GROUP_W)
_REF_OFF = np.concatenate([[0], np.cumsum(_REF_WIDTHS)]).tolist()
_NAMES = ("a_x", "a_z", "b_q", "b_k", "b_v", "b_gl", "b_z", "c_q", "c_f", "c_i", "c_z",
          "d_b", "d_c", "d_x", "d_z", "e_q", "e_z")
_PACK_ORDER = ("a_x", "a_z", "b_q", "b_k", "b_v", "b_z", "c_q", "c_f", "c_i", "c_z",
               "d_b", "d_c", "d_x", "d_z", "e_q", "e_z", "b_gl")
_OFF = {}
_o = 0
for _n in _PACK_ORDER:
    _w = _REF_WIDTHS[_NAMES.index(_n)]
    _OFF[_n] = (_o, _o + _w)
    _o += -(-_w // LANES) * LANES
D_PROJ_P = _o

PROMPT_CHUNK = 64
SAMPLE_CHUNK = 8
PROMPT_SEQS = 4
SAMPLE_SEQS = 8
ROW_TILE_IN = 1024
VMEM_LIMIT = 48 * 1024 * 1024
PROJ_DTYPE = BF16


def _bf(x):
    return x.astype(BF16)


def _mm(a, b):
    return lax.dot_general(_bf(a), _bf(b), (((1,), (0,)), ((), ())), preferred_element_type=F32)


def _mm_nt(a, b):
    return lax.dot_general(_bf(a), _bf(b), (((1,), (1,)), ((), ())), preferred_element_type=F32)


def _mm_tn(a, b):
    return lax.dot_general(_bf(a), _bf(b), (((0,), (0,)), ((), ())), preferred_element_type=F32)


def _rows(shape):
    return lax.broadcasted_iota(jnp.int32, shape, 0)


def _cols(shape):
    return lax.broadcasted_iota(jnp.int32, shape, 1)


def _shift_rows(x, s):
    if s == 0:
        return x
    return jnp.where(_rows(x.shape) >= s, pltpu.roll(x, s, axis=0), 0.0)


def _cumsum_rows(x):
    s = 1
    while s < x.shape[0]:
        x = x + _shift_rows(x, s)
        s *= 2
    return x


def _softplus(x):
    return jnp.maximum(x, 0.0) + jnp.log(1.0 + jnp.exp(-jnp.abs(x)))


def _log_sigmoid(x):
    return -_softplus(-x)


def _silu(x):
    return x * jax.nn.sigmoid(x)


def _head_sum(x, ones_bd):
    hi = _bf(x)
    lo = _bf(x - hi.astype(F32))
    return (jnp.dot(hi, ones_bd, preferred_element_type=F32)
            + jnp.dot(lo, ones_bd, preferred_element_type=F32))


def _log2(n):
    return int(np.log2(n))


def _rms_matmul_kernel(w_transposed, x_ref, g_ref, w_ref, *o_refs):
    x = x_ref[...]
    u = _bf(x * lax.rsqrt(jnp.mean(x * x, axis=-1, keepdims=True) + EPS) * g_ref[...])
    off = 0
    for o_ref in o_refs:
        n = o_ref.shape[-1]
        if w_transposed:
            y = lax.dot_general(u, w_ref[off:off + n, :], (((1,), (1,)), ((), ())),
                                preferred_element_type=F32)
        else:
            y = jnp.dot(u, w_ref[:, off:off + n], preferred_element_type=F32)
        o_ref[...] = y.astype(o_ref.dtype)
        off += n


def _rms_matmul(x, g_all, w_all, layer, out_widths, name, w_transposed, out_dtype):
    n, d = x.shape
    tm = min(ROW_TILE_IN, n)
    return pl.pallas_call(
        functools.partial(_rms_matmul_kernel, w_transposed),
        grid=(n // tm,),
        in_specs=[pl.BlockSpec((tm, d), lambda i: (i, 0)),
                  pl.BlockSpec((None, 1, d), lambda i: (layer, 0, 0)),
                  pl.BlockSpec((None,) + w_all.shape[1:], lambda i: (layer, 0, 0),
                               pipeline_mode=pl.Buffered(1))],
        out_specs=[pl.BlockSpec((tm, ow), lambda i: (i, 0)) for ow in out_widths],
        out_shape=[jax.ShapeDtypeStruct((n, ow), out_dtype) for ow in out_widths],
        compiler_params=pltpu.CompilerParams(dimension_semantics=("arbitrary",),
                                             vmem_limit_bytes=VMEM_LIMIT),
        name=name,
    )(x, g_all.reshape(g_all.shape[0], 1, d), w_all)


def _pack_w_in_kernel(w_ref, o_ref):
    for name in _PACK_ORDER:
        i = _NAMES.index(name)
        lo, _ = _OFF[name]
        o_ref[lo:lo + _REF_WIDTHS[i], :] = _bf(w_ref[_REF_OFF[i]:_REF_OFF[i + 1], :])
    end = _OFF[_PACK_ORDER[-1]][1]
    o_ref[end:, :] = jnp.zeros((D_PROJ_P - end, o_ref.shape[1]), BF16)


def _pack_w_in(w_in_t):
    n_l, n_out, d = w_in_t.shape
    tk = 2 * LANES
    return pl.pallas_call(
        _pack_w_in_kernel,
        grid=(n_l, d // tk),
        in_specs=[pl.BlockSpec((None, n_out, tk), lambda l, j: (l, 0, j))],
        out_specs=pl.BlockSpec((None, D_PROJ_P, tk), lambda l, j: (l, 0, j)),
        out_shape=jax.ShapeDtypeStruct((n_l, D_PROJ_P, d), BF16),
        compiler_params=pltpu.CompilerParams(dimension_semantics=("arbitrary", "arbitrary"),
                                             vmem_limit_bytes=VMEM_LIMIT),
        name="pack_w_in",
    )(w_in_t)


def _interleave(tasks):
    results = [None] * len(tasks)
    live = list(range(len(tasks)))
    while live:
        still = []
        for i in live:
            try:
                next(tasks[i])
                still.append(i)
            except StopIteration as done:
                results[i] = done.value
        live = still
    return results


def _rwkv_head(r, v, k, at, rt, bt, kt, bh, kh, e_end_c, s0):
    c = r.shape[0]
    ri, ci = _rows((c, c)), _cols((c, c))
    strict, incl = ci < ri, ci <= ri
    ar = jnp.concatenate([at, rt], axis=0)
    gb = _mm_nt(ar, bt)
    gk = _mm_nt(ar, kt)
    w0 = _mm_nt(ar, s0)
    yield
    u = w0[:c] + _mm(jnp.where(strict, gk[:c], 0.0), v)
    p, n = jnp.where(strict, gb[:c], 0.0), 1
    yield
    while n < c:
        u = u + _mm(p, u)
        n *= 2
        if n < c:
            p = _mm(p, p)
        yield
    o = w0[c:] + _mm(jnp.where(incl, gb[c:], 0.0), u) + _mm(jnp.where(incl, gk[c:], 0.0), v)
    s_new = s0 * e_end_c + _mm_tn(u, bh) + _mm_tn(v, kh)
    return o, s_new


def _gla_chunk(q, k, v, lg, st, dk, ones_bd):
    c, wk = q.shape
    b = _cumsum_rows(lg)
    b_c = b[c - 1:c, :]
    o = _mm_nt(q * jnp.exp(b), st)
    rows = _rows((c, wk))
    sj_head = lax.shift_right_logical(_rows((GROUP_W, wk)), _log2(HEAD_DIM))
    sk_head = lax.shift_right_logical(_cols((GROUP_W, wk)), _log2(dk))
    st_new = st * jnp.exp(b_c) + jnp.where(sj_head == sk_head, _mm_tn(v, k * jnp.exp(b_c - b)), 0.0)

    in_blk = jnp.bitwise_and(rows, SUBLANES - 1)
    terms = [q * k]
    for d in range(1, SUBLANES):
        kd, bd = pltpu.roll(k, d, axis=0), pltpu.roll(b, d, axis=0)
        terms.append(jnp.where(in_blk >= d, q * kd * jnp.exp(jnp.minimum(b - bd, 0.0)), 0.0))
    s_all = _mm(jnp.concatenate(terms, axis=0), ones_bd)
    yield

    if c == SUBLANES:
        for d in range(SUBLANES):
            o = o + s_all[d * c:(d + 1) * c] * (v if d == 0 else pltpu.roll(v, d, axis=0))
        return o, st_new

    assert c == HEAD_DIM, "the attention matrix shares the 64-lane head segments of ones_bd"
    ai, aj = _rows((c, N_HEADS * c)), jnp.bitwise_and(_cols((c, N_HEADS * c)), c - 1)
    lag = ai - aj
    att = jnp.where(lag == 0, s_all[:c], 0.0)
    for d in range(1, SUBLANES):
        att = jnp.where(lag == d, s_all[d * c:(d + 1) * c], att)
    half = c // 2
    hj_head = lax.shift_right_logical(_rows((N_HEADS * c, wk)), _log2(c))
    k_head = lax.shift_right_logical(_cols((N_HEADS * c, wk)), _log2(dk))
    kmask = hj_head == k_head
    while half >= SUBLANES:
        blk = 2 * half
        nb = c // blk
        mid = jnp.broadcast_to(b.reshape(nb, blk, wk)[:, half - 1:half, :], (nb, blk, wk)).reshape(c, wk)
        second = jnp.bitwise_and(rows, blk - 1) >= half
        ql = jnp.where(second, q * jnp.exp(jnp.where(second, b - mid, 0.0)), 0.0)
        kl = jnp.where(second, 0.0, k * jnp.exp(jnp.where(second, 0.0, mid - b)))
        kbd = jnp.where(kmask, jnp.concatenate([kl] * N_HEADS, axis=0), 0.0)
        lvl = _mm_nt(ql, kbd)
        same = lax.shift_right_logical(ai, _log2(blk)) == lax.shift_right_logical(aj, _log2(blk))
        lvl = jnp.where(same, lvl, 0.0)
        att = att + lvl
        half //= 2
        yield
    vj_head = lax.shift_right_logical(_rows((N_HEADS * c, GROUP_W)), _log2(c))
    vv_head = lax.shift_right_logical(_cols((N_HEADS * c, GROUP_W)), _log2(HEAD_DIM))
    vbd = jnp.where(vj_head == vv_head, jnp.concatenate([v] * N_HEADS, axis=0), 0.0)
    o = o + _mm(att, vbd)
    yield
    return o, st_new


def _mem_attention(qe, kbd, vbd):
    sc = _mm(qe, kbd)
    yield
    prs = []
    for h in range(N_HEADS):
        s_h = sc[:, h * N_MEM:(h + 1) * N_MEM]
        e_h = jnp.exp(s_h - jnp.max(s_h, axis=-1, keepdims=True))
        prs.append(e_h / jnp.sum(e_h, axis=-1, keepdims=True))
    oe = _mm_nt(jnp.concatenate(prs, axis=-1), vbd)
    yield
    return oe


def _mixers_kernel(t_valid, nc,
                   proj_ref, xres_ref, wout_ref, lnpost_ref, mk_ref, mv_ref,
                   wkv0_ref, shift0_ref, gla0_ref, hgrn0_ref, conv0_ref,
                   mu_ref, w0_ref, w2_ref, a0_ref, a2_ref, akk_ref, aka_ref, ark_ref, agg_ref, agb_ref,
                   g2_ref, g2b_ref, bgg_ref, lb_ref, cgg_ref, cw_ref, gsum_ref, hsum_ref,
                   y_ref, wkv_ref, shift_ref, gla_ref, hgrn_ref, conv_ref,
                   kbd_ref, vbd_ref, glat_ref, hgrnt_ref):
    n_seq, c = proj_ref.shape[0], proj_ref.shape[1]
    step = pl.program_id(1)

    @pl.when(jnp.logical_and(pl.program_id(0) == 0, step == 0))
    def _():
        glat_ref[...] = jnp.zeros(glat_ref.shape, F32)
        hgrnt_ref[...] = jnp.zeros(hgrnt_ref.shape, F32)
        kbd_ref[...] = jnp.zeros(kbd_ref.shape, BF16)
        vbd_ref[...] = jnp.zeros(vbd_ref.shape, BF16)

    @pl.when(step == 0)
    def _():
        wkv_ref[...] = wkv0_ref[...]
        shift_ref[...] = shift0_ref[...]
        conv_ref[...] = conv0_ref[...]
        for g in range(n_seq):
            for h in range(N_HEADS):
                rs = slice(h * HEAD_DIM, (h + 1) * HEAD_DIM)
                ms = slice(h * N_MEM, (h + 1) * N_MEM)
                kbd_ref[g, rs, ms] = _bf(mk_ref[g, h])
                vbd_ref[g, rs, ms] = _bf(mv_ref[g, h])
                glat_ref[g, rs, h * B_DK:(h + 1) * B_DK] = gla0_ref[g, h].T
                hgrnt_ref[g, rs, h * C_EXPAND:(h + 1) * C_EXPAND] = hgrn0_ref[g, h].T

    row1 = _rows((c, 1))
    pad = t_valid < c
    valid = row1 < t_valid
    last = (t_valid if pad else c) - 1

    def prepare(g, st):
        def seg(name):
            lo, hi = _OFF[name]
            return proj_ref[g, :, lo:hi].astype(F32)

        ax = seg("a_x")
        prev = _shift_rows(ax, 1) + jnp.where(row1 == 0, st["shift"], 0.0)
        amix = ax + (prev - ax) * mu_ref[...]
        r = amix[:, 0:GROUP_W]
        k = amix[:, GROUP_W:2 * GROUP_W]
        v = amix[:, 2 * GROUP_W:3 * GROUP_W]
        wl = amix[:, 3 * GROUP_W:3 * GROUP_W + A_LORA]
        al = amix[:, 3 * GROUP_W + A_LORA:A_SHIFT]
        w_raw = -_softplus(-(w0_ref[...] + _mm(jnp.tanh(wl), w2_ref[...]))) - 0.5
        lw = -jnp.exp(w_raw)
        a = jax.nn.sigmoid(a0_ref[...] + _mm(al, a2_ref[...]))
        kk = k * akk_ref[...]
        kk = kk * lax.rsqrt(jnp.maximum(_head_sum(kk * kk, hsum_ref[...]), 1e-24))
        k = k * (1.0 + (a - 1.0) * aka_ref[...])
        bb = kk * a
        if pad:
            lw = jnp.where(valid, lw, 0.0)
            bb = jnp.where(valid, bb, 0.0)
            k = jnp.where(valid, k, 0.0)
        cum = _cumsum_rows(lw)
        cum_c = cum[c - 1:c, :]
        e_cum, e_inv, e_end = jnp.exp(cum), jnp.exp(-cum), jnp.exp(cum_c - cum)
        at = -kk * jnp.exp(cum - lw)
        rt = r * e_cum
        bt, kt = bb * e_inv, k * e_inv
        bh, kh = bb * e_end, k * e_end
        e_end_c = jnp.exp(cum_c)
        tasks = []
        for h in range(N_HEADS):
            sl = slice(h * HEAD_DIM, (h + 1) * HEAD_DIM)
            tasks.append(_rwkv_head(r[:, sl], v[:, sl], k[:, sl], at[:, sl], rt[:, sl], bt[:, sl],
                                    kt[:, sl], bh[:, sl], kh[:, sl], e_end_c[:, sl], st["wkv"][h]))

        bq = seg("b_q") * (B_DK ** -0.5)
        bk = seg("b_k")
        lo, _ = _OFF["b_gl"]
        lg = (_log_sigmoid(_mm(proj_ref[g, :, lo:lo + LANES].astype(F32), g2_ref[...]) + g2b_ref[...])
              * (1.0 / B_GATE_NORM))
        if pad:
            lg = jnp.where(valid, lg, 0.0)
            bk = jnp.where(valid, bk, 0.0)
        tasks.append(_gla_chunk(bq, bk, seg("b_v"), lg, st["gla"], B_DK, gsum_ref[...]))

        lb = lb_ref[...]
        cq = _silu(seg("c_q")) * (C_EXPAND ** -0.5)
        cf = seg("c_f")
        la, lbb = jnp.log(lb), jnp.log1p(-lb) + _log_sigmoid(cf)
        log_f = jnp.maximum(la, lbb) + jnp.log(1.0 + jnp.exp(-jnp.abs(la - lbb)))
        kc = (1.0 - lb) * jax.nn.sigmoid(-cf)
        if pad:
            log_f = jnp.where(valid, log_f, 0.0)
            kc = jnp.where(valid, kc, 0.0)
        tasks.append(_gla_chunk(cq, kc, seg("c_i"), log_f, st["hgrn"], C_EXPAND, hsum_ref[...]))

        tasks.append(_mem_attention(seg("e_q") * (HEAD_DIM ** -0.5), kbd_ref[g], vbd_ref[g]))

        def finish(res):
            new = {"wkv": [s_h for _, s_h in res[:N_HEADS]], "shift": ax[last:last + 1, :]}
            o_a = jnp.concatenate([o_h for o_h, _ in res[:N_HEADS]], axis=-1)
            mean = _head_sum(o_a, hsum_ref[...]) * (1.0 / HEAD_DIM)
            var = _head_sum(jnp.square(o_a - mean), hsum_ref[...]) * (1.0 / HEAD_DIM)
            o_a = (o_a - mean) * lax.rsqrt(var + A_GN_EPS) * agg_ref[...] + agb_ref[...]
            bonus = _head_sum(r * k * ark_ref[...], hsum_ref[...]) * v
            out_a = (o_a + bonus) * _silu(seg("a_z"))

            o_b, new["gla"] = res[N_HEADS]
            o_b = o_b * lax.rsqrt(_head_sum(o_b * o_b, hsum_ref[...]) * (1.0 / HEAD_DIM) + EPS)
            out_b = o_b * bgg_ref[...] * _silu(seg("b_z"))

            o_c, new["hgrn"] = res[N_HEADS + 1]
            o_c = o_c * lax.rsqrt(jnp.mean(o_c * o_c, axis=-1, keepdims=True) + EPS) * cgg_ref[...]
            out_c = o_c * _silu(seg("c_z"))

            cu = seg("d_c") * seg("d_x")
            p1 = _shift_rows(cu, 1) + jnp.where(row1 == 0, st["conv"][1:2, :], 0.0)
            p2 = (_shift_rows(cu, 2) + jnp.where(row1 == 0, st["conv"][0:1, :], 0.0)
                  + jnp.where(row1 == 1, st["conv"][1:2, :], 0.0))
            conv = p2 * cw_ref[0:1, :] + p1 * cw_ref[1:2, :] + cu * cw_ref[2:3, :]
            out_d = seg("d_b") * conv * _silu(seg("d_z"))
            new["conv"] = cu[last - 1:last + 1, :]

            out_e = res[N_HEADS + 2] * _silu(seg("e_z"))
            new["mix"] = jnp.concatenate([out_a, out_b, out_c, out_d, out_e], axis=-1)
            return new

        return tasks, finish

    olds = [{"wkv": [wkv_ref[g, h] for h in range(N_HEADS)], "shift": shift_ref[g], "gla": glat_ref[g],
             "hgrn": hgrnt_ref[g], "conv": conv_ref[g]} for g in range(n_seq)]
    prepared = [prepare(g, olds[g]) for g in range(n_seq)]
    n_tasks = N_HEADS + 3
    results = _interleave([t for tasks, _ in prepared for t in tasks])
    mixes = []
    for g, (_, finish) in enumerate(prepared):
        new = finish(results[g * n_tasks:(g + 1) * n_tasks])
        for h in range(N_HEADS):
            wkv_ref[g, h] = new["wkv"][h]
        shift_ref[g] = new["shift"]
        glat_ref[g] = new["gla"]
        hgrnt_ref[g] = new["hgrn"]
        conv_ref[g] = new["conv"]
        mixes.append(new["mix"])

    z = jnp.dot(_bf(jnp.concatenate(mixes, axis=0)), wout_ref[...], preferred_element_type=F32)
    z = z * lax.rsqrt(jnp.mean(z * z, axis=-1, keepdims=True) + EPS) * lnpost_ref[...]
    y_ref[...] = xres_ref[...] + z.reshape(n_seq, c, D_MODEL)

    @pl.when(step == nc - 1)
    def _():
        for g in range(n_seq):
            for h in range(N_HEADS):
                rs = slice(h * HEAD_DIM, (h + 1) * HEAD_DIM)
                gla_ref[g, h] = glat_ref[g, rs, h * B_DK:(h + 1) * B_DK].T
                hgrn_ref[g, h] = hgrnt_ref[g, rs, h * C_EXPAND:(h + 1) * C_EXPAND].T


def _mixers(proj, xres, w_out_all, ln_post_all, layer, mem_kt, mem_vt, mem_layer, wkv0, shift0, gla0, hgrn0,
            conv0, params, chunk, t_valid, n_seq, name):
    s, t, _ = proj.shape
    nc = t // chunk
    seq3 = lambda i, j: (i, 0, 0)
    seq4 = lambda i, j: (i, 0, 0, 0)
    mem5 = lambda i, j: (mem_layer, i, 0, 0, 0)
    par = lambda i, j: (0, 0)
    state_blocks = [((n_seq, N_HEADS, HEAD_DIM, HEAD_DIM), seq4),
                    ((n_seq, 1, A_SHIFT), seq3),
                    ((n_seq, N_HEADS, B_DK, HEAD_DIM), seq4),
                    ((n_seq, N_HEADS, C_EXPAND, HEAD_DIM), seq4),
                    ((n_seq, D_CONV_W - 1, GROUP_W), seq3)]
    in_specs = [pl.BlockSpec((n_seq, chunk, D_PROJ_P), lambda i, j: (i, j, 0)),
                pl.BlockSpec((n_seq, chunk, D_MODEL), lambda i, j: (i, j, 0)),
                pl.BlockSpec((None, D_MIX, D_MODEL), lambda i, j: (layer, 0, 0), pipeline_mode=pl.Buffered(1)),
                pl.BlockSpec((None, 1, D_MODEL), lambda i, j: (layer, 0, 0)),
                pl.BlockSpec((None, n_seq, N_HEADS, HEAD_DIM, N_MEM), mem5),
                pl.BlockSpec((None, n_seq, N_HEADS, HEAD_DIM, N_MEM), mem5)]
    in_specs += [pl.BlockSpec(b, m) for b, m in state_blocks]
    in_specs += [pl.BlockSpec(p.shape, par) for p in params]
    out_specs = [pl.BlockSpec((n_seq, chunk, D_MODEL), lambda i, j: (i, j, 0))]
    out_specs += [pl.BlockSpec(b, m) for b, m in state_blocks]
    out_shape = [jax.ShapeDtypeStruct((s, t, D_MODEL), F32)]
    out_shape += [jax.ShapeDtypeStruct((s,) + b[1:], F32) for b, _ in state_blocks]
    return pl.pallas_call(
        functools.partial(_mixers_kernel, t_valid, nc),
        grid=(s // n_seq, nc),
        in_specs=in_specs,
        out_specs=out_specs,
        out_shape=out_shape,
        scratch_shapes=[pltpu.VMEM((n_seq, GROUP_W, N_HEADS * N_MEM), BF16),
                        pltpu.VMEM((n_seq, GROUP_W, N_HEADS * N_MEM), BF16),
                        pltpu.VMEM((n_seq, GROUP_W, B_QK), F32),
                        pltpu.VMEM((n_seq, GROUP_W, C_F), F32)],
        compiler_params=pltpu.CompilerParams(dimension_semantics=("arbitrary", "arbitrary"),
                                             vmem_limit_bytes=VMEM_LIMIT),
        name=name,
    )(proj, xres, w_out_all, ln_post_all.reshape(ln_post_all.shape[0], 1, D_MODEL), mem_kt, mem_vt,
      wkv0, shift0, gla0, hgrn0, conv0, *params)


def kernel(x_prompt, x_sample, mem_prompt, state_a_wkv, state_a_shift, state_b_gla, state_c_hgrn,
           state_d_conv, cache_mem_k, cache_mem_v, ln_pre, ln_post, w_in, w_out, a_mu, a_w0, a_w2,
           a_a0, a_a2, a_kk, a_ka, a_rk, a_gn_g, a_gn_b, b_g2, b_g2_bias, b_gn_g, c_lb_logits,
           c_gn_g, d_conv_w, e_mem_g, e_wk, e_wv):
    bp, tp, _ = x_prompt.shape
    bs, ts, _ = x_sample.shape
    assert tp % PROMPT_CHUNK == 0 and ts <= SAMPLE_CHUNK and ts >= D_CONV_W - 1
    assert bp % PROMPT_SEQS == 0 and bs % SAMPLE_SEQS == 0

    w_in_pt = _pack_w_in(jnp.transpose(w_in, (0, 2, 1)))
    w_out_b = _bf(w_out)
    cache_kt = jnp.transpose(cache_mem_k, (0, 1, 3, 4, 2))
    cache_vt = jnp.transpose(cache_mem_v, (0, 1, 3, 4, 2))
    w_kv = _bf(jnp.concatenate([e_wk, e_wv], axis=-1))
    g2_p = jnp.concatenate([b_g2, jnp.zeros((DEPTH, LANES - B_LORA_G, B_QK), b_g2.dtype)], axis=1)
    lb_cum = jnp.cumsum(jax.nn.softmax(c_lb_logits.astype(F32), axis=0), axis=0)
    lb_all = lb_cum - lb_cum[:1]

    def row(p):
        return p.reshape(1, -1)

    def head_blocks(rows_per_head):
        rh = np.arange(N_HEADS * rows_per_head)[:, None] // rows_per_head
        return jnp.asarray(rh == np.arange(GROUP_W)[None, :] // HEAD_DIM, dtype=BF16)

    gsum, hsum = head_blocks(B_DK), head_blocks(HEAD_DIM)

    zp = lambda *sh: jnp.zeros((bp,) + sh, F32)
    ys_pad = jnp.pad(x_sample, ((0, 0), (0, SAMPLE_CHUNK - ts), (0, 0))).reshape(bs * SAMPLE_CHUNK, D_MODEL)
    yp = x_prompt.reshape(bp * tp, D_MODEL)
    mem2 = mem_prompt.reshape(bp * N_MEM, D_MODEL)

    outs = [[] for _ in range(12)]
    for l in range(DEPTH):
        params = [row(a_mu[l]), row(a_w0[l]), a_w2[l], row(a_a0[l]), a_a2[l], row(a_kk[l]), row(a_ka[l]),
                  row(a_rk[l]), row(a_gn_g[l]), row(a_gn_b[l]), g2_p[l], row(b_g2_bias[l]),
                  row(b_gn_g[l]), row(lb_all[l]), row(c_gn_g[l]), d_conv_w[l], gsum, hsum]
        mk, mv = _rms_matmul(mem2, e_mem_g, w_kv, l, (GROUP_W, GROUP_W), f"memkv{l}", False, F32)
        mkt = jnp.transpose(mk.reshape(bp, N_MEM, N_HEADS, HEAD_DIM), (0, 2, 3, 1))
        mvt = jnp.transpose(mv.reshape(bp, N_MEM, N_HEADS, HEAD_DIM), (0, 2, 3, 1))

        (proj_p,) = _rms_matmul(yp, ln_pre, w_in_pt, l, (D_PROJ_P,), f"inproj_p{l}", True, PROJ_DTYPE)
        yp3, p_wkv, p_shift, p_gla, p_hgrn, p_conv = _mixers(
            proj_p.reshape(bp, tp, D_PROJ_P), yp.reshape(bp, tp, D_MODEL), w_out_b, ln_post, l,
            mkt[None], mvt[None], 0,
            zp(N_HEADS, HEAD_DIM, HEAD_DIM), zp(1, A_SHIFT), zp(N_HEADS, B_DK, HEAD_DIM),
            zp(N_HEADS, C_EXPAND, HEAD_DIM), zp(D_CONV_W - 1, GROUP_W), params,
            PROMPT_CHUNK, PROMPT_CHUNK, PROMPT_SEQS, f"mixers_p{l}")
        yp = yp3.reshape(bp * tp, D_MODEL)

        (proj_s,) = _rms_matmul(ys_pad, ln_pre, w_in_pt, l, (D_PROJ_P,), f"inproj_s{l}", True, PROJ_DTYPE)
        ys3, s_wkv, s_shift, s_gla, s_hgrn, s_conv = _mixers(
            proj_s.reshape(bs, SAMPLE_CHUNK, D_PROJ_P), ys_pad.reshape(bs, SAMPLE_CHUNK, D_MODEL),
            w_out_b, ln_post, l, cache_kt, cache_vt, l,
            state_a_wkv[l], state_a_shift[l].reshape(bs, 1, A_SHIFT),
            state_b_gla[l], state_c_hgrn[l], state_d_conv[l],
            params, SAMPLE_CHUNK, ts, SAMPLE_SEQS, f"mixers_s{l}")
        ys_pad = ys3.reshape(bs * SAMPLE_CHUNK, D_MODEL)

        for lst, val in zip(outs, (
                p_wkv, p_shift.reshape(bp, A_SHIFT), p_gla, p_hgrn, p_conv, mkt, mvt,
                s_wkv, s_shift.reshape(bs, A_SHIFT), s_gla, s_hgrn, s_conv)):
            lst.append(val)

    y_prompt = yp.reshape(bp, tp, D_MODEL)
    y_sample = ys_pad.reshape(bs, SAMPLE_CHUNK, D_MODEL)[:, :ts]
    stacked = [jnp.stack(o) for o in outs]
    for i in (5, 6):
        stacked[i] = jnp.transpose(stacked[i], (0, 1, 4, 2, 3))
    return (y_prompt, y_sample) + tuple(stacked)
```
